```python
import math
import jax, jax.numpy as jnp
from jax import lax
import numpy as np

D_MODEL = 1024
BATCH = 2
SEQ = 8192
DEPTH = 1

HEAD_DIM = 64
DIL_GROUPS = ((128, 1), (512, 4), (2048, 16))
N_GROUPS = 3
HEADS_PER_GROUP = 4
N_ATTN_HEADS = N_GROUPS * HEADS_PER_GROUP
ATTN_W = N_ATTN_HEADS * HEAD_DIM
ATTN_OUT_W = HEADS_PER_GROUP * HEAD_DIM
BLK = 128
SGU_CHUNK = 128
SGU_GROUPS = 4
SGU_W = 512
SGU_GROUP_W = SGU_W // SGU_GROUPS
N_BRANCH = 2
IN_W = 3 * ATTN_W + 2 * SGU_W + N_BRANCH * D_MODEL
MEM_LEN = 256
MEM_HEADS = 4
MEM_HEAD_DIM = 128
MEM_W = MEM_HEADS * MEM_HEAD_DIM
D_FF = -(-8 * D_MODEL // (3 * 256)) * 256
EPS = 1e-6

kernel_name = "hybrid_dilated_attn_sgu_gated_block"


def rmsnorm(x, g):
    xf = x.astype(jnp.float32)
    r = lax.rsqrt(jnp.mean(xf * xf, axis=-1, keepdims=True) + EPS)
    return (xf * r * g.astype(jnp.float32)).astype(x.dtype)


def alibi_slopes_grouped():
    def pow2(n):
        start = 2.0 ** (-8.0 / n)
        return [start ** (i + 1) for i in range(n)]
    n = N_ATTN_HEADS
    if math.log2(n).is_integer():
        s = pow2(n)
    else:
        c = 2 ** int(math.floor(math.log2(n)))
        s = pow2(c) + pow2(2 * c)[0::2][: n - c]
    s = np.array(sorted(s, reverse=True), dtype=np.float32)
    return s.reshape(N_GROUPS, HEADS_PER_GROUP)


def dilated_causal_window_attention(q, k, v, slopes, window, dilation):
    B, S, H, Dh = q.shape
    n_back = window // dilation
    assert n_back <= BLK
    sub_len = -(-S // dilation)
    L = -(-sub_len // BLK) * BLK
    S_pad = L * dilation
    nb = L // BLK

    def to_blocks(t):
        t = jnp.pad(t, ((0, 0), (0, S_pad - S), (0, 0), (0, 0)))
        t = t.reshape(B, nb, BLK, dilation, H, Dh)
        return t.transpose(0, 3, 4, 1, 2, 5)

    def with_prev(t):
        prev = jnp.pad(t, ((0, 0), (0, 0), (0, 0), (1, 0), (0, 0), (0, 0)))[:, :, :, :-1]
        return jnp.concatenate([prev, t], axis=4)

    qb = to_blocks(q)
    kk = with_prev(to_blocks(k))
    vv = with_prev(to_blocks(v))

    s = jnp.einsum('brhnqd,brhnkd->brhnqk', qb, kk).astype(jnp.float32) * (Dh ** -0.5)
    steps = (np.arange(BLK)[:, None] + BLK) - np.arange(2 * BLK)[None, :]
    band = (steps >= 0) & (steps <= n_back)
    no_prev = (np.arange(nb)[:, None, None] == 0) & (np.arange(2 * BLK)[None, None, :] < BLK)
    valid = band[None] & ~no_prev
    dist = jnp.asarray((np.clip(steps, 0, None) * dilation).astype(np.float32))
    bias = -jnp.asarray(slopes)[:, None, None, None] * dist[None, None]
    s = jnp.where(jnp.asarray(valid), s + bias, -jnp.inf)
    mx = jnp.max(s, axis=-1, keepdims=True)
    e = jnp.exp(s - mx)
    den = jnp.sum(e, axis=-1, keepdims=True)
    o = jnp.einsum('brhnqk,brhnkd->brhnqd', e, vv.astype(jnp.float32)) / den
    lse = (mx + jnp.log(den))[..., 0]
    o = o.transpose(0, 3, 4, 1, 2, 5).reshape(B, S_pad, H, Dh)[:, :S]
    lse = lse.transpose(0, 3, 4, 1, 2).reshape(B, S_pad, H)[:, :S]
    return o, lse


def spatial_gating(uv, w_s, b_s, g):
    B, S, _ = uv.shape
    z = jax.nn.gelu(uv)
    u, v = jnp.split(z, 2, axis=-1)
    v = rmsnorm(v, g)
    v = v.reshape(B, S // SGU_CHUNK, SGU_CHUNK, SGU_GROUPS, SGU_GROUP_W)
    mixed = jnp.einsum('gts,bnsgc->bntgc', jnp.tril(w_s), v) + b_s.T[:, :, None]
    return u * mixed.reshape(B, S, SGU_W)


def memory_cross_attention(c, m, w_q, w_kv, w_o):
    B, S, _ = c.shape
    M = m.shape[1]
    q = (c @ w_q).reshape(B, S, MEM_HEADS, MEM_HEAD_DIM)
    kv = (m @ w_kv).reshape(B, M, 2, MEM_HEADS, MEM_HEAD_DIM)
    k, v = kv[:, :, 0], kv[:, :, 1]
    s = jnp.einsum('bshd,bmhd->bhsm', q, k).astype(jnp.float32) * (MEM_HEAD_DIM ** -0.5)
    p = jax.nn.softmax(s, axis=-1)
    o = jnp.einsum('bhsm,bmhd->bshd', p, v.astype(jnp.float32)).astype(c.dtype)
    return o.reshape(B, S, MEM_W) @ w_o


def setup_inputs(seed: int = 0) -> dict:
    key = jax.random.key(seed)
    ks = jax.random.split(key, 24)
    f32 = jnp.float32

    def nrm(k, shape, scale):
        return jax.random.normal(k, shape, f32) * scale

    def gain(k, shape):
        return 1.0 + 0.02 * jax.random.normal(k, shape, f32)

    L = DEPTH
    return {
        "x": nrm(ks[0], (BATCH, SEQ, D_MODEL), 1.0),
        "mem": nrm(ks[1], (BATCH, MEM_LEN, D_MODEL), 1.0),
        "g_mix": gain(ks[2], (L, D_MODEL)),
        "w_in": nrm(ks[3], (L, D_MODEL, IN_W), D_MODEL ** -0.5),
        "b_gate": nrm(ks[4], (L, N_BRANCH * D_MODEL), 0.1),
        "w_sgu_spatial": nrm(ks[5], (L, SGU_GROUPS, SGU_CHUNK, SGU_CHUNK), 0.5 * SGU_CHUNK ** -0.5),
        "b_sgu_spatial": 1.0 + nrm(ks[6], (L, SGU_GROUPS, SGU_CHUNK), 0.1),
        "g_sgu": gain(ks[7], (L, SGU_W)),
        "w_branch_attn": nrm(ks[8], (L, ATTN_OUT_W, D_MODEL), ATTN_OUT_W ** -0.5),
        "w_branch_sgu": nrm(ks[9], (L, SGU_W, D_MODEL), SGU_W ** -0.5),
        "w_out": nrm(ks[10], (L, D_MODEL, D_MODEL), D_MODEL ** -0.5),
        "g_cross": gain(ks[11], (L, D_MODEL)),
        "g_mem": gain(ks[12], (L, D_MODEL)),
        "w_q_cross": nrm(ks[13], (L, D_MODEL, MEM_W), D_MODEL ** -0.5),
        "w_kv_cross": nrm(ks[14], (L, D_MODEL, 2 * MEM_W), D_MODEL ** -0.5),
        "w_o_cross": nrm(ks[15], (L, MEM_W, D_MODEL), MEM_W ** -0.5),
        "g_ffn": gain(ks[16], (L, D_MODEL)),
        "w_gate_up": nrm(ks[17], (L, D_MODEL, 2 * D_FF), D_MODEL ** -0.5),
        "w_down": nrm(ks[18], (L, D_FF, D_MODEL), D_FF ** -0.5),
        "g_final": gain(ks[19], (D_MODEL,)),
    }


def reference(x, mem, g_mix, w_in, b_gate, w_sgu_spatial, b_sgu_spatial, g_sgu,
              w_branch_attn, w_branch_sgu, w_out, g_cross, g_mem, w_q_cross,
              w_kv_cross, w_o_cross, g_ffn, w_gate_up, w_down, g_final):
    B, S, D = x.shape
    slopes = alibi_slopes_grouped()
    h = x
    for l in range(DEPTH):
        a = rmsnorm(h, g_mix[l])
        proj = a @ w_in[l]
        q, k, v, uv, gl = jnp.split(
            proj, [ATTN_W, 2 * ATTN_W, 3 * ATTN_W, 3 * ATTN_W + 2 * SGU_W], axis=-1)
        q = q.reshape(B, S, N_GROUPS, HEADS_PER_GROUP, HEAD_DIM)
        k = k.reshape(B, S, N_GROUPS, HEADS_PER_GROUP, HEAD_DIM)
        v = v.reshape(B, S, N_GROUPS, HEADS_PER_GROUP, HEAD_DIM)
        outs, lses = [], []
        for gi, (win, dil) in enumerate(DIL_GROUPS):
            o, lse = dilated_causal_window_attention(
                q[:, :, gi], k[:, :, gi], v[:, :, gi], slopes[gi], win, dil)
            outs.append(o)
            lses.append(lse)
        outs = jnp.stack(outs)
        alpha = jax.nn.softmax(jnp.stack(lses), axis=0)
        y_attn = jnp.sum(alpha[..., None] * outs, axis=0).reshape(B, S, ATTN_OUT_W).astype(x.dtype)

        y_sgu = spatial_gating(uv, w_sgu_spatial[l], b_sgu_spatial[l], g_sgu[l])

        gates = jax.nn.sigmoid((gl + b_gate[l]).astype(jnp.float32)).astype(x.dtype)
        gates = gates.reshape(B, S, N_BRANCH, D)
        merged = (gates[:, :, 0] * (y_attn @ w_branch_attn[l])
                  + gates[:, :, 1] * (y_sgu @ w_branch_sgu[l]))
        h = h + merged @ w_out[l]

        c = rmsnorm(h, g_cross[l])
        m = rmsnorm(mem, g_mem[l])
        h = h + memory_cross_attention(c, m, w_q_cross[l], w_kv_cross[l], w_o_cross[l])

        f = rmsnorm(h, g_ffn[l])
        gt, up = jnp.split(f @ w_gate_up[l], 2, axis=-1)
        h = h + (jax.nn.silu(gt) * up) @ w_down[l]
    return rmsnorm(h, g_final)
```

```python
import functools
import math

import jax
import jax.numpy as jnp
import numpy as np
from jax import lax
from jax.experimental import pallas as pl
from jax.experimental.pallas import tpu as pltpu

D_MODEL = 1024
HEAD_DIM = 64
DIL_GROUPS = ((128, 1), (512, 4), (2048, 16))
N_GROUPS = 3
HEADS_PER_GROUP = 4
N_ATTN_HEADS = N_GROUPS * HEADS_PER_GROUP
ATTN_W = N_ATTN_HEADS * HEAD_DIM
GROUP_W = HEADS_PER_GROUP * HEAD_DIM
BLK = 128
LANES = 128
SGU_CHUNK = 128
SGU_GROUPS = 4
SGU_W = 512
SGU_GROUP_W = SGU_W // SGU_GROUPS
N_BRANCH = 2
MEM_HEADS = 4
MEM_HEAD_DIM = 128
MEM_W = MEM_HEADS * MEM_HEAD_DIM
D_FF = 2816
EPS = 1e-6

ATTN_TILE = 2048
TM = 512
FF_CHUNK = 1408
VMEM_LIMIT = 48 * 1024 * 1024

F32 = jnp.float32
BF16 = jnp.bfloat16


def _rms(xf, g):
    r = lax.rsqrt(jnp.mean(xf * xf, axis=-1, keepdims=True) + EPS)
    return xf * r * g


def _alibi_slopes_grouped():
    def pow2(n):
        start = 2.0 ** (-8.0 / n)
        return [start ** (i + 1) for i in range(n)]
    n = N_ATTN_HEADS
    if math.log2(n).is_integer():
        s = pow2(n)
    else:
        c = 2 ** int(math.floor(math.log2(n)))
        s = pow2(c) + pow2(2 * c)[0::2][: n - c]
    s = np.array(sorted(s, reverse=True), dtype=np.float32)
    return s.reshape(N_GROUPS, HEADS_PER_GROUP)


def _attn_bias(gi):
    window, dil = DIL_GROUPS[gi]
    n_back = window // dil
    steps = (np.arange(BLK)[:, None] + BLK) - np.arange(2 * BLK)[None, :]
    band = (steps >= 0) & (steps <= n_back)
    dist = (np.clip(steps, 0, None) * dil).astype(np.float32)
    slopes = _alibi_slopes_grouped()[gi]
    bias = -slopes[:, None, None] * dist[None]
    bias = np.where(band[None], bias, -np.inf).astype(np.float32)
    return bias.reshape(HEADS_PER_GROUP * BLK, 2 * BLK)


def _const_spec(shape):
    nd = len(shape)
    return pl.BlockSpec(shape, lambda *_: (0,) * nd, pipeline_mode=pl.Buffered(1))


def _qkv_kernel(x_ref, g_ref, w_ref, *refs):
    out_refs, scr = refs[:-1], refs[-1]
    a = _rms(x_ref[...], g_ref[...]).astype(BF16)
    res = jnp.dot(a, w_ref[...], preferred_element_type=F32)
    for c in range(3 * ATTN_W // LANES):
        scr[c] = res[:, c * LANES:(c + 1) * LANES]
    for which in range(3):
        for gi, (_, dil) in enumerate(DIL_GROUPS):
            slab = (which * ATTN_W + gi * GROUP_W) // LANES
            out = out_refs[gi * 3 + which]
            for r in range(dil):
                for half in range(GROUP_W // LANES):
                    val = scr[slab + half, pl.ds(r, TM // dil, stride=dil), :]
                    if which == 0:
                        val = val * (HEAD_DIM ** -0.5)
                    lo = r * GROUP_W + half * LANES
                    out[:, lo:lo + LANES] = val.astype(BF16)


def _qkv_proj(x2, g_mix, w_qkv):
    T = x2.shape[0]
    out_shapes, out_specs = [], []
    for _, dil in DIL_GROUPS:
        for _ in range(3):
            out_shapes.append(jax.ShapeDtypeStruct((T // dil, dil * GROUP_W), BF16))
            out_specs.append(pl.BlockSpec((TM // dil, dil * GROUP_W), lambda i: (i, 0)))
    return pl.pallas_call(
        _qkv_kernel,
        grid=(T // TM,),
        in_specs=[pl.BlockSpec((TM, D_MODEL), lambda i: (i, 0)),
                  _const_spec((1, D_MODEL)),
                  _const_spec((D_MODEL, 3 * ATTN_W))],
        out_specs=out_specs,
        out_shape=out_shapes,
        scratch_shapes=[pltpu.VMEM((3 * ATTN_W // LANES, TM, LANES), F32)],
        compiler_params=pltpu.CompilerParams(
            dimension_semantics=("arbitrary",), vmem_limit_bytes=VMEM_LIMIT),
        name="qkv_proj",
    )(x2, g_mix, w_qkv)


def _attn_kernel(q_ref, k_ref, kp_ref, v_ref, vp_ref, bias_ref, o_ref, lse_ref, *, dil):
    first_tile = pl.program_id(1) == 0
    rows = ATTN_TILE // dil
    lane = lax.broadcasted_iota(jnp.int32, (1, GROUP_W), 1)
    head_masks = [(lane >= HEAD_DIM * h) & (lane < HEAD_DIM * (h + 1))
                  for h in range(HEADS_PER_GROUP)]
    key_col = lax.broadcasted_iota(jnp.int32, (1, 2 * BLK), 1)
    no_prev = jnp.logical_and(first_tile, key_col < BLK)

    def by_head(parts):
        out = parts[HEADS_PER_GROUP - 1]
        for h in range(HEADS_PER_GROUP - 2, -1, -1):
            out = jnp.where(head_masks[h], parts[h], out)
        return out

    for r in range(dil):
        lanes = slice(r * GROUP_W, (r + 1) * GROUP_W)
        for i in range(rows // BLK):
            cur = slice(i * BLK, (i + 1) * BLK)
            q = q_ref[cur, lanes]
            if i == 0:
                kp, vp = kp_ref[:, lanes], vp_ref[:, lanes]
            else:
                prev = slice((i - 1) * BLK, i * BLK)
                kp, vp = k_ref[prev, lanes], v_ref[prev, lanes]
            k2 = jnp.concatenate([kp, k_ref[cur, lanes]], axis=0)
            v2 = jnp.concatenate([vp, v_ref[cur, lanes]], axis=0)
            zero = jnp.zeros_like(q)
            qs = jnp.concatenate([jnp.where(m, q, zero) for m in head_masks], axis=0)
            s = lax.dot_general(qs, k2, (((1,), (1,)), ((), ())),
                                preferred_element_type=F32)
            s = s + bias_ref[...]
            if i == 0:
                s = jnp.where(no_prev, -jnp.inf, s)
            mx = jnp.max(s, axis=-1, keepdims=True)
            e = jnp.exp(s - mx)
            den = jnp.sum(e, axis=-1, keepdims=True)
            pv = jnp.dot(e.astype(BF16), v2, preferred_element_type=F32)
            pv = pv * (1.0 / den)
            lse = mx + jnp.log(den)
            o = by_head([pv[h * BLK:(h + 1) * BLK] for h in range(HEADS_PER_GROUP)])
            l = by_head([jnp.broadcast_to(lse[h * BLK:(h + 1) * BLK], (BLK, GROUP_W))
                         for h in range(HEADS_PER_GROUP)])
            dst = cur if dil == 1 else pl.ds(i * BLK * dil + r, BLK, stride=dil)
            for half in range(GROUP_W // LANES):
                o_ref[half, dst, :] = o[:, half * LANES:(half + 1) * LANES]
                lse_ref[half, dst, :] = l[:, half * LANES:(half + 1) * LANES]


def _dilated_attention(gi, qd, kd, vd, batch):
    _, dil = DIL_GROUPS[gi]
    T = qd.shape[0] * dil
    tiles = T // ATTN_TILE
    tiles_per_batch = tiles // batch
    rows = ATTN_TILE // dil
    width = dil * GROUP_W
    prev_blocks = rows // BLK

    def cur_map(b, t):
        return (b * tiles_per_batch + t, 0)

    def prev_map(b, t):
        return (jnp.maximum((b * tiles_per_batch + t) * prev_blocks - 1, 0), 0)

    def out_map(b, t):
        return (0, b * tiles_per_batch + t, 0)

    bias = jnp.asarray(_attn_bias(gi))
    return pl.pallas_call(
        functools.partial(_attn_kernel, dil=dil),
        grid=(batch, tiles_per_batch),
        in_specs=[pl.BlockSpec((rows, width), cur_map),
                  pl.BlockSpec((rows, width), cur_map),
                  pl.BlockSpec((BLK, width), prev_map),
                  pl.BlockSpec((rows, width), cur_map),
                  pl.BlockSpec((BLK, width), prev_map),
                  _const_spec(bias.shape)],
        out_specs=[pl.BlockSpec((GROUP_W // LANES, ATTN_TILE, LANES), out_map)] * 2,
        out_shape=[jax.ShapeDtypeStruct((GROUP_W // LANES, T, LANES), F32)] * 2,
        compiler_params=pltpu.CompilerParams(
            dimension_semantics=("arbitrary", "arbitrary"), vmem_limit_bytes=VMEM_LIMIT),
        name=f"dilated_attn_g{gi}",
    )(qd, kd, kd, vd, vd, bias)


def _mixer_kernel(x_ref, o0_ref, o1_ref, o2_ref, l0_ref, l1_ref, l2_ref,
                  g_ref, w_ref, bg_ref, ws_ref, bs_ref, gs_ref,
                  wba_ref, wbs_ref, wo_ref, h_ref):
    x = x_ref[...]
    a = _rms(x, g_ref[...]).astype(BF16)
    proj = jnp.dot(a, w_ref[...], preferred_element_type=F32)

    z = jax.nn.gelu(proj[:, :2 * SGU_W], approximate=True)
    u = z[:, :SGU_W]
    v = _rms(z[:, SGU_W:], gs_ref[...]).astype(BF16)
    row = lax.broadcasted_iota(jnp.int32, (SGU_CHUNK, SGU_CHUNK), 0)
    col = lax.broadcasted_iota(jnp.int32, (SGU_CHUNK, SGU_CHUNK), 1)
    ws = [jnp.where(row >= col, ws_ref[g], 0.0).astype(BF16) for g in range(SGU_GROUPS)]
    chunks = []
    for c in range(TM // SGU_CHUNK):
        rs = slice(c * SGU_CHUNK, (c + 1) * SGU_CHUNK)
        parts = [jnp.dot(ws[g], v[rs, g * SGU_GROUP_W:(g + 1) * SGU_GROUP_W],
                         preferred_element_type=F32) for g in range(SGU_GROUPS)]
        chunks.append(jnp.concatenate(parts, axis=1) + bs_ref[...])
    y_sgu = u * jnp.concatenate(chunks, axis=0)

    def halves(ref):
        return jnp.concatenate([ref[half] for half in range(GROUP_W // LANES)], axis=1)

    l0, l1, l2 = halves(l0_ref), halves(l1_ref), halves(l2_ref)
    lmax = jnp.maximum(jnp.maximum(l0, l1), l2)
    e0, e1, e2 = jnp.exp(l0 - lmax), jnp.exp(l1 - lmax), jnp.exp(l2 - lmax)
    y_attn = ((e0 * halves(o0_ref) + e1 * halves(o1_ref) + e2 * halves(o2_ref))
              * (1.0 / (e0 + e1 + e2)))

    gates = 1.0 / (1.0 + jnp.exp(-(proj[:, 2 * SGU_W:] + bg_ref[...])))
    merged = (gates[:, :D_MODEL] * jnp.dot(y_attn.astype(BF16), wba_ref[...],
                                            preferred_element_type=F32)
              + gates[:, D_MODEL:] * jnp.dot(y_sgu.astype(BF16), wbs_ref[...],
                                             preferred_element_type=F32))
    h_ref[...] = x + jnp.dot(merged.astype(BF16), wo_ref[...], preferred_element_type=F32)


def _mixer(x2, outs, lses, g_mix, w_uvgl, b_gate, w_s, bs_b, g_sgu, w_ba, w_bs, w_out):
    T = x2.shape[0]
    row_spec = lambda w: pl.BlockSpec((TM, w), lambda i: (i, 0))
    half_spec = pl.BlockSpec((GROUP_W // LANES, TM, LANES), lambda i: (0, i, 0))
    consts = (g_mix, w_uvgl, b_gate, w_s, bs_b, g_sgu, w_ba, w_bs, w_out)
    return pl.pallas_call(
        _mixer_kernel,
        grid=(T // TM,),
        in_specs=([row_spec(D_MODEL)] + [half_spec] * 6
                  + [_const_spec(c.shape) for c in consts]),
        out_specs=row_spec(D_MODEL),
        out_shape=jax.ShapeDtypeStruct((T, D_MODEL), F32),
        compiler_params=pltpu.CompilerParams(
            dimension_semantics=("arbitrary",), vmem_limit_bytes=VMEM_LIMIT),
        name="mixer_tail",
    )(x2, *outs, *lses, *consts)


def _mem_kv_kernel(m_ref, g_ref, w_ref, kv_ref):
    m = _rms(m_ref[...], g_ref[...]).astype(BF16)
    kv_ref[...] = jnp.dot(m, w_ref[...], preferred_element_type=F32).astype(BF16)


def _mem_kv(mem2, g_mem, w_kv):
    rows = mem2.shape[0]
    return pl.pallas_call(
        _mem_kv_kernel,
        grid=(1,),
        in_specs=[_const_spec(mem2.shape), _const_spec(g_mem.shape), _const_spec(w_kv.shape)],
        out_specs=pl.BlockSpec((rows, 2 * MEM_W), lambda i: (0, 0)),
        out_shape=jax.ShapeDtypeStruct((rows, 2 * MEM_W), BF16),
        compiler_params=pltpu.CompilerParams(
            dimension_semantics=("arbitrary",), vmem_limit_bytes=VMEM_LIMIT),
        name="mem_kv",
    )(mem2, g_mem, w_kv)


def _cross_kernel(h_ref, kv_ref, g_ref, wq_ref, wo_ref, out_ref):
    h = h_ref[...]
    c = _rms(h, g_ref[...]).astype(BF16)
    q = jnp.dot(c, wq_ref[...], preferred_element_type=F32).astype(BF16)
    heads = []
    for hd in range(MEM_HEADS):
        cols = slice(hd * MEM_HEAD_DIM, (hd + 1) * MEM_HEAD_DIM)
        k = kv_ref[:, cols]
        v = kv_ref[:, MEM_W + hd * MEM_HEAD_DIM:MEM_W + (hd + 1) * MEM_HEAD_DIM]
        s = lax.dot_general(q[:, cols], k, (((1,), (1,)), ((), ())),
                            preferred_element_type=F32) * (MEM_HEAD_DIM ** -0.5)
        e = jnp.exp(s - jnp.max(s, axis=-1, keepdims=True))
        den = jnp.sum(e, axis=-1, keepdims=True)
        o = jnp.dot(e.astype(BF16), v, preferred_element_type=F32) * (1.0 / den)
        heads.append(o.astype(BF16))
    o = jnp.concatenate(heads, axis=1)
    out_ref[...] = h + jnp.dot(o, wo_ref[...], preferred_element_type=F32)


def _cross_attention(h, kv, g_cross, w_q, w_o, batch):
    T = h.shape[0]
    blocks_per_batch = T // batch // TM
    mem_len = kv.shape[0] // batch
    return pl.pallas_call(
        _cross_kernel,
        grid=(batch, blocks_per_batch),
        in_specs=[pl.BlockSpec((TM, D_MODEL), lambda b, i: (b * blocks_per_batch + i, 0)),
                  pl.BlockSpec((mem_len, 2 * MEM_W), lambda b, i: (b, 0)),
                  _const_spec(g_cross.shape), _const_spec(w_q.shape), _const_spec(w_o.shape)],
        out_specs=pl.BlockSpec((TM, D_MODEL), lambda b, i: (b * blocks_per_batch + i, 0)),
        out_shape=jax.ShapeDtypeStruct((T, D_MODEL), F32),
        compiler_params=pltpu.CompilerParams(
            dimension_semantics=("arbitrary", "arbitrary"), vmem_limit_bytes=VMEM_LIMIT),
        name="cross_attn",
    )(h, kv, g_cross, w_q, w_o)


def _ffn_kernel(h_ref, g_ref, wgu_ref, wd_ref, gf_ref, out_ref):
    h = h_ref[...]
    f = _rms(h, g_ref[...]).astype(BF16)
    acc = h
    for c in range(D_FF // FF_CHUNK):
        gt = jnp.dot(f, wgu_ref[:, c * FF_CHUNK:(c + 1) * FF_CHUNK],
                     preferred_element_type=F32)
        up = jnp.dot(f, wgu_ref[:, D_FF + c * FF_CHUNK:D_FF + (c + 1) * FF_CHUNK],
                     preferred_element_type=F32)
        act = (gt * (1.0 / (1.0 + jnp.exp(-gt))) * up).astype(BF16)
        acc = acc + jnp.dot(act, wd_ref[c * FF_CHUNK:(c + 1) * FF_CHUNK, :],
                            preferred_element_type=F32)
    out_ref[...] = _rms(acc, gf_ref[...])


def _ffn(h, g_ffn, w_gu, w_down, g_final):
    T = h.shape[0]
    consts = (g_ffn, w_gu, w_down, g_final)
    return pl.pallas_call(
        _ffn_kernel,
        grid=(T // TM,),
        in_specs=[pl.BlockSpec((TM, D_MODEL), lambda i: (i, 0))]
                 + [_const_spec(c.shape) for c in consts],
        out_specs=pl.BlockSpec((TM, D_MODEL), lambda i: (i, 0)),
        out_shape=jax.ShapeDtypeStruct((T, D_MODEL), F32),
        compiler_params=pltpu.CompilerParams(
            dimension_semantics=("arbitrary",), vmem_limit_bytes=VMEM_LIMIT),
        name="ffn",
    )(h, *consts)


def kernel(x, mem, g_mix, w_in, b_gate, w_sgu_spatial, b_sgu_spatial, g_sgu, w_branch_attn,
           w_branch_sgu, w_out, g_cross, g_mem, w_q_cross, w_kv_cross, w_o_cross, g_ffn,
           w_gate_up, w_down, g_final):
    B, S, D = x.shape
    assert D == D_MODEL and S % ATTN_TILE == 0 and w_in.shape[0] == 1
    T = B * S
    x2 = x.reshape(T, D)
    row = lambda v: v.reshape(1, -1)

    w_in_b = w_in[0].astype(BF16)
    w_qkv, w_uvgl = w_in_b[:, :3 * ATTN_W], w_in_b[:, 3 * ATTN_W:]
    g_mix2 = row(g_mix[0])

    qkv = _qkv_proj(x2, g_mix2, w_qkv)
    outs, lses = [], []
    for gi in range(N_GROUPS):
        o, l = _dilated_attention(gi, *qkv[3 * gi:3 * gi + 3], batch=B)
        outs.append(o)
        lses.append(l)

    bs_b = jnp.repeat(b_sgu_spatial[0].T, SGU_GROUP_W, axis=1)
    h = _mixer(x2, outs, lses, g_mix2, w_uvgl, row(b_gate[0]), w_sgu_spatial[0], bs_b,
               row(g_sgu[0]), w_branch_attn[0].astype(BF16), w_branch_sgu[0].astype(BF16),
               w_out[0].astype(BF16))

    kv = _mem_kv(mem.reshape(B * mem.shape[1], D), row(g_mem[0]), w_kv_cross[0].astype(BF16))
    h = _cross_attention(h, kv, row(g_cross[0]), w_q_cross[0].astype(BF16),
                         w_o_cross[0].astype(BF16), batch=B)

    out = _ffn(h, row(g_ffn[0]), w_gate_up[0].astype(BF16), w_down[0].astype(BF16),
               row(g_final))
    return out.reshape(B, S, D)
```

```python
import math

import jax
import jax.numpy as jnp
import numpy as np
from jax import lax
from jax.experimental import pallas as pl
from jax.experimental.pallas import tpu as pltpu

D_MODEL = 1024
HEAD_DIM = 64
DIL_GROUPS = ((128, 1), (512, 4), (2048, 16))
N_GROUPS = 3
HEADS_PER_GROUP = 4
N_ATTN_HEADS = N_GROUPS * HEADS_PER_GROUP
ATTN_W = N_ATTN_HEADS * HEAD_DIM
GROUP_W = HEADS_PER_GROUP * HEAD_DIM
BLK = 128
LANES = 128
HALVES = GROUP_W // LANES
SGU_CHUNK = 128
SGU_GROUPS = 4
SGU_W = 512
SGU_GROUP_W = SGU_W // SGU_GROUPS
N_BRANCH = 2
MEM_HEADS = 4
MEM_HEAD_DIM = 128
MEM_W = MEM_HEADS * MEM_HEAD_DIM
D_FF = 2816
EPS = 1e-6
LOG2E = math.log2(math.e)

ATTN_TILE = 2048
TM = 512
TM_FFN = 1024
SUB = 512
FF_CHUNK = 1408
VMEM_LIMIT = 48 * 1024 * 1024
VMEM_LIMIT_BIG = 56 * 1024 * 1024

F32 = jnp.float32
BF16 = jnp.bfloat16


def _rms(xf, g):
    r = lax.rsqrt(jnp.mean(xf * xf, axis=-1, keepdims=True) + EPS)
    return xf * r * g


def _alibi_slopes_grouped():
    def pow2(n):
        start = 2.0 ** (-8.0 / n)
        return [start ** (i + 1) for i in range(n)]
    n = N_ATTN_HEADS
    if math.log2(n).is_integer():
        s = pow2(n)
    else:
        c = 2 ** int(math.floor(math.log2(n)))
        s = pow2(c) + pow2(2 * c)[0::2][: n - c]
    s = np.array(sorted(s, reverse=True), dtype=np.float32)
    return s.reshape(N_GROUPS, HEADS_PER_GROUP)


def _attn_bias(gi):
    window, dil = DIL_GROUPS[gi]
    n_back = window // dil
    steps = (np.arange(BLK)[:, None] + BLK) - np.arange(2 * BLK)[None, :]
    band = (steps >= 0) & (steps <= n_back)
    dist = (np.clip(steps, 0, None) * dil).astype(np.float32)
    slopes = _alibi_slopes_grouped()[gi]
    bias = -slopes[:, None, None] * dist[None]
    bias = np.where(band[None], bias.astype(np.float64) * LOG2E, -np.inf).astype(np.float32)
    return bias.reshape(HEADS_PER_GROUP * BLK, 2 * BLK)


def _const_spec(shape):
    nd = len(shape)
    return pl.BlockSpec(shape, lambda *_: (0,) * nd, pipeline_mode=pl.Buffered(1))


def _qkv_kernel(x_ref, g_ref, w_ref, *refs):
    out_refs, scr = refs[:-1], refs[-1]
    a = _rms(x_ref[...], g_ref[...]).astype(BF16)
    res = jnp.dot(a, w_ref[...], preferred_element_type=F32)
    for c in range(3 * ATTN_W // LANES):
        scr[c] = res[:, c * LANES:(c + 1) * LANES]
    for which in range(3):
        for gi, (_, dil) in enumerate(DIL_GROUPS):
            slab = (which * ATTN_W + gi * GROUP_W) // LANES
            out = out_refs[gi * 3 + which]
            for r in range(dil):
                for half in range(HALVES):
                    val = scr[slab + half, pl.ds(r, TM // dil, stride=dil), :]
                    if which == 0:
                        val = val * (HEAD_DIM ** -0.5 * LOG2E)
                    lo = r * GROUP_W + half * LANES
                    out[:, lo:lo + LANES] = val.astype(BF16)


def _qkv_proj(x2, g_mix, w_qkv):
    T = x2.shape[0]
    out_shapes, out_specs = [], []
    for _, dil in DIL_GROUPS:
        for _ in range(3):
            out_shapes.append(jax.ShapeDtypeStruct((T // dil, dil * GROUP_W), BF16))
            out_specs.append(pl.BlockSpec((TM // dil, dil * GROUP_W), lambda i: (i, 0)))
    return pl.pallas_call(
        _qkv_kernel,
        grid=(T // TM,),
        in_specs=[pl.BlockSpec((TM, D_MODEL), lambda i: (i, 0)),
                  _const_spec((1, D_MODEL)),
                  _const_spec((D_MODEL, 3 * ATTN_W))],
        out_specs=out_specs,
        out_shape=out_shapes,
        scratch_shapes=[pltpu.VMEM((3 * ATTN_W // LANES, TM, LANES), F32)],
        compiler_params=pltpu.CompilerParams(
            dimension_semantics=("arbitrary",), vmem_limit_bytes=VMEM_LIMIT),
        name="qkv_proj",
    )(x2, g_mix, w_qkv)


def _attn_block(q, k2, v2, bias, no_prev, head_masks, low_lanes):
    zero = jnp.zeros_like(q)
    qs = jnp.concatenate([jnp.where(m, q, zero) for m in head_masks], axis=0)
    s = lax.dot_general(qs, k2, (((1,), (1,)), ((), ())),
                        preferred_element_type=F32) + bias
    if no_prev is not None:
        s = jnp.where(no_prev, -jnp.inf, s)
    mx = jnp.max(s, axis=-1, keepdims=True)
    e = jnp.exp2(s - mx)
    den = jnp.sum(e, axis=-1, keepdims=True)
    pv = jnp.dot(e.astype(BF16), v2, preferred_element_type=F32)

    def head_rows(x, h):
        return x[h * BLK:(h + 1) * BLK]

    acc, m_rep, l_rep = [], [], []
    for half in range(HALVES):
        ha, hb = 2 * half, 2 * half + 1
        cols = slice(half * LANES, (half + 1) * LANES)
        acc.append(jnp.where(low_lanes, head_rows(pv, ha)[:, cols], head_rows(pv, hb)[:, cols]))
        m_rep.append(jnp.where(low_lanes, head_rows(mx, ha), head_rows(mx, hb)))
        l_rep.append(jnp.where(low_lanes, head_rows(den, ha), head_rows(den, hb)))
    return acc, m_rep, l_rep


def _attn_kernel(*refs):
    in_refs, y_ref, nat = refs[:6 * N_GROUPS], refs[6 * N_GROUPS], refs[6 * N_GROUPS + 1]
    first_tile = pl.program_id(1) == 0
    lane = lax.broadcasted_iota(jnp.int32, (1, GROUP_W), 1)
    head_masks = [(lane >= HEAD_DIM * h) & (lane < HEAD_DIM * (h + 1))
                  for h in range(HEADS_PER_GROUP)]
    low_lanes = lax.broadcasted_iota(jnp.int32, (1, LANES), 1) < HEAD_DIM
    key_col = lax.broadcasted_iota(jnp.int32, (1, 2 * BLK), 1)
    no_prev_first = jnp.logical_and(first_tile, key_col < BLK)

    def group_blocks(gi):
        q_ref, k_ref, kp_ref, v_ref, vp_ref, bias_ref = in_refs[6 * gi:6 * gi + 6]
        dil = DIL_GROUPS[gi][1]
        for r in range(dil):
            lanes = slice(r * GROUP_W, (r + 1) * GROUP_W)
            for i in range(ATTN_TILE // dil // BLK):
                cur = slice(i * BLK, (i + 1) * BLK)
                if i == 0:
                    kp, vp = kp_ref[:, lanes], vp_ref[:, lanes]
                else:
                    prev = slice((i - 1) * BLK, i * BLK)
                    kp, vp = k_ref[prev, lanes], v_ref[prev, lanes]
                k2 = jnp.concatenate([kp, k_ref[cur, lanes]], axis=0)
                v2 = jnp.concatenate([vp, v_ref[cur, lanes]], axis=0)
                yield r, i, _attn_block(q_ref[cur, lanes], k2, v2, bias_ref[...],
                                        no_prev_first if i == 0 else None,
                                        head_masks, low_lanes)

    for gi in range(1, N_GROUPS):
        dil = DIL_GROUPS[gi][1]
        for r, i, stats in group_blocks(gi):
            dst = pl.ds(i * BLK * dil + r, BLK, stride=dil)
            for kind, parts in enumerate(stats):
                for half in range(HALVES):
                    nat[gi - 1, kind, half, dst, :] = parts[half]

    for r, i, (acc0, m0, l0) in group_blocks(0):
        rows = slice(i * BLK, (i + 1) * BLK)
        for half in range(HALVES):
            accs = [acc0[half]] + [nat[g, 0, half, rows, :] for g in range(N_GROUPS - 1)]
            ms = [m0[half]] + [nat[g, 1, half, rows, :] for g in range(N_GROUPS - 1)]
            ls = [l0[half]] + [nat[g, 2, half, rows, :] for g in range(N_GROUPS - 1)]
            m_all = jnp.maximum(jnp.maximum(ms[0], ms[1]), ms[2])
            ws = [jnp.exp2(m - m_all) for m in ms]
            num = ws[0] * accs[0] + ws[1] * accs[1] + ws[2] * accs[2]
            den = ws[0] * ls[0] + ws[1] * ls[1] + ws[2] * ls[2]
            y_ref[rows, half * LANES:(half + 1) * LANES] = (num * (1.0 / den)).astype(BF16)


def _dilated_attention(qkv, batch):
    T = qkv[0].shape[0]
    tiles_per_batch = T // ATTN_TILE // batch

    def cur_map(b, t):
        return (b * tiles_per_batch + t, 0)

    operands, in_specs = [], []
    for gi, (_, dil) in enumerate(DIL_GROUPS):
        qd, kd, vd = qkv[3 * gi:3 * gi + 3]
        rows, width = ATTN_TILE // dil, dil * GROUP_W
        prev_blocks = rows // BLK

        def prev_map(b, t, prev_blocks=prev_blocks):
            return (jnp.maximum((b * tiles_per_batch + t) * prev_blocks - 1, 0), 0)

        bias = jnp.asarray(_attn_bias(gi))
        operands += [qd, kd, kd, vd, vd, bias]
        in_specs += [pl.BlockSpec((rows, width), cur_map),
                     pl.BlockSpec((rows, width), cur_map),
                     pl.BlockSpec((BLK, width), prev_map),
                     pl.BlockSpec((rows, width), cur_map),
                     pl.BlockSpec((BLK, width), prev_map),
                     _const_spec(bias.shape)]
    return pl.pallas_call(
        _attn_kernel,
        grid=(batch, tiles_per_batch),
        in_specs=in_specs,
        out_specs=pl.BlockSpec((ATTN_TILE, GROUP_W), cur_map),
        out_shape=jax.ShapeDtypeStruct((T, GROUP_W), BF16),
        scratch_shapes=[pltpu.VMEM((N_GROUPS - 1, 3, HALVES, ATTN_TILE, LANES), F32)],
        compiler_params=pltpu.CompilerParams(
            dimension_semantics=("arbitrary", "arbitrary"), vmem_limit_bytes=VMEM_LIMIT),
        name="dilated_attn",
    )(*operands)


def _mixer_kernel(x_ref, y_ref, g_ref, w_ref, bg_ref, ws_ref, bs_ref, gs_ref,
                  wba_ref, wbs_ref, wo_ref, h_ref):
    x = x_ref[...]
    a = _rms(x, g_ref[...]).astype(BF16)
    proj = jnp.dot(a, w_ref[...], preferred_element_type=F32)

    z = jax.nn.gelu(proj[:, :2 * SGU_W], approximate=True)
    u = z[:, :SGU_W]
    v = _rms(z[:, SGU_W:], gs_ref[...]).astype(BF16)
    row = lax.broadcasted_iota(jnp.int32, (SGU_CHUNK, SGU_CHUNK), 0)
    col = lax.broadcasted_iota(jnp.int32, (SGU_CHUNK, SGU_CHUNK), 1)
    ws = [jnp.where(row >= col, ws_ref[g], 0.0).astype(BF16) for g in range(SGU_GROUPS)]
    chunks = []
    for c in range(TM // SGU_CHUNK):
        rs = slice(c * SGU_CHUNK, (c + 1) * SGU_CHUNK)
        parts = [jnp.dot(ws[g], v[rs, g * SGU_GROUP_W:(g + 1) * SGU_GROUP_W],
                         preferred_element_type=F32) for g in range(SGU_GROUPS)]
        chunks.append(jnp.concatenate(parts, axis=1) + bs_ref[...])
    y_sgu = u * jnp.concatenate(chunks, axis=0)

    gates = 1.0 / (1.0 + jnp.exp(-(proj[:, 2 * SGU_W:] + bg_ref[...])))
    merged = (gates[:, :D_MODEL] * jnp.dot(y_ref[...], wba_ref[...],
                                            preferred_element_type=F32)
              + gates[:, D_MODEL:] * jnp.dot(y_sgu.astype(BF16), wbs_ref[...],
                                             preferred_element_type=F32))
    h_ref[...] = x + jnp.dot(merged.astype(BF16), wo_ref[...], preferred_element_type=F32)


def _mixer(x2, y_attn, g_mix, w_uvgl, b_gate, w_s, bs_b, g_sgu, w_ba, w_bs, w_out):
    T = x2.shape[0]
    row_spec = lambda w: pl.BlockSpec((TM, w), lambda i: (i, 0))
    consts = (g_mix, w_uvgl, b_gate, w_s, bs_b, g_sgu, w_ba, w_bs, w_out)
    return pl.pallas_call(
        _mixer_kernel,
        grid=(T // TM,),
        in_specs=([row_spec(D_MODEL), row_spec(GROUP_W)]
                  + [_const_spec(c.shape) for c in consts]),
        out_specs=row_spec(D_MODEL),
        out_shape=jax.ShapeDtypeStruct((T, D_MODEL), F32),
        compiler_params=pltpu.CompilerParams(
            dimension_semantics=("arbitrary",), vmem_limit_bytes=VMEM_LIMIT),
        name="mixer_tail",
    )(x2, y_attn, *consts)


def _mem_kv_kernel(m_ref, g_ref, w_ref, kv_ref):
    m = _rms(m_ref[...], g_ref[...]).astype(BF16)
    kv_ref[...] = jnp.dot(m, w_ref[...], preferred_element_type=F32).astype(BF16)


def _mem_kv(mem2, g_mem, w_kv):
    rows = mem2.shape[0]
    return pl.pallas_call(
        _mem_kv_kernel,
        grid=(1,),
        in_specs=[_const_spec(mem2.shape), _const_spec(g_mem.shape), _const_spec(w_kv.shape)],
        out_specs=pl.BlockSpec((rows, 2 * MEM_W), lambda i: (0, 0)),
        out_shape=jax.ShapeDtypeStruct((rows, 2 * MEM_W), BF16),
        compiler_params=pltpu.CompilerParams(
            dimension_semantics=("arbitrary",), vmem_limit_bytes=VMEM_LIMIT),
        name="mem_kv",
    )(mem2, g_mem, w_kv)


def _cross_kernel(h_ref, kv_ref, g_ref, wq_ref, wo_ref, out_ref):
    h = h_ref[...]
    c = _rms(h, g_ref[...]).astype(BF16)
    q = jnp.dot(c, wq_ref[...], preferred_element_type=F32).astype(BF16)
    heads = []
    for hd in range(MEM_HEADS):
        cols = slice(hd * MEM_HEAD_DIM, (hd + 1) * MEM_HEAD_DIM)
        k = kv_ref[:, cols]
        v = kv_ref[:, MEM_W + hd * MEM_HEAD_DIM:MEM_W + (hd + 1) * MEM_HEAD_DIM]
        s = lax.dot_general(q[:, cols], k, (((1,), (1,)), ((), ())),
                            preferred_element_type=F32) * (MEM_HEAD_DIM ** -0.5)
        e = jnp.exp(s - jnp.max(s, axis=-1, keepdims=True))
        den = jnp.sum(e, axis=-1, keepdims=True)
        o = jnp.dot(e.astype(BF16), v, preferred_element_type=F32) * (1.0 / den)
        heads.append(o.astype(BF16))
    o = jnp.concatenate(heads, axis=1)
    out_ref[...] = h + jnp.dot(o, wo_ref[...], preferred_element_type=F32)


def _cross_attention(h, kv, g_cross, w_q, w_o, batch):
    T = h.shape[0]
    blocks_per_batch = T // batch // TM
    mem_len = kv.shape[0] // batch
    return pl.pallas_call(
        _cross_kernel,
        grid=(batch, blocks_per_batch),
        in_specs=[pl.BlockSpec((TM, D_MODEL), lambda b, i: (b * blocks_per_batch + i, 0)),
                  pl.BlockSpec((mem_len, 2 * MEM_W), lambda b, i: (b, 0)),
                  _const_spec(g_cross.shape), _const_spec(w_q.shape), _const_spec(w_o.shape)],
        out_specs=pl.BlockSpec((TM, D_MODEL), lambda b, i: (b * blocks_per_batch + i, 0)),
        out_shape=jax.ShapeDtypeStruct((T, D_MODEL), F32),
        compiler_params=pltpu.CompilerParams(
            dimension_semantics=("arbitrary", "arbitrary"), vmem_limit_bytes=VMEM_LIMIT),
        name="cross_attn",
    )(h, kv, g_cross, w_q, w_o)


def _ffn_kernel(h_ref, g_ref, wgu_ref, wd_ref, gf_ref, out_ref):
    for part in range(TM_FFN // SUB):
        rows = slice(part * SUB, (part + 1) * SUB)
        h = h_ref[rows, :]
        f = _rms(h, g_ref[...]).astype(BF16)
        acc = h
        for c in range(D_FF // FF_CHUNK):
            gt = jnp.dot(f, wgu_ref[:, c * FF_CHUNK:(c + 1) * FF_CHUNK],
                         preferred_element_type=F32)
            up = jnp.dot(f, wgu_ref[:, D_FF + c * FF_CHUNK:D_FF + (c + 1) * FF_CHUNK],
                         preferred_element_type=F32)
            act = (gt * (1.0 / (1.0 + jnp.exp(-gt))) * up).astype(BF16)
            acc = acc + jnp.dot(act, wd_ref[c * FF_CHUNK:(c + 1) * FF_CHUNK, :],
                                preferred_element_type=F32)
        out_ref[rows, :] = _rms(acc, gf_ref[...])


def _ffn(h, g_ffn, w_gu, w_down, g_final):
    T = h.shape[0]
    consts = (g_ffn, w_gu, w_down, g_final)
    return pl.pallas_call(
        _ffn_kernel,
        grid=(T // TM_FFN,),
        in_specs=[pl.BlockSpec((TM_FFN, D_MODEL), lambda i: (i, 0))]
                 + [_const_spec(c.shape) for c in consts],
        out_specs=pl.BlockSpec((TM_FFN, D_MODEL), lambda i: (i, 0)),
        out_shape=jax.ShapeDtypeStruct((T, D_MODEL), F32),
        compiler_params=pltpu.CompilerParams(
            dimension_semantics=("arbitrary",), vmem_limit_bytes=VMEM_LIMIT_BIG),
        name="ffn",
    )(h, *consts)


def kernel(x, mem, g_mix, w_in, b_gate, w_sgu_spatial, b_sgu_spatial, g_sgu, w_branch_attn,
           w_branch_sgu, w_out, g_cross, g_mem, w_q_cross, w_kv_cross, w_o_cross, g_ffn,
           w_gate_up, w_down, g_final):
    B, S, D = x.shape
    assert D == D_MODEL and S % ATTN_TILE == 0 and w_in.shape[0] == 1
    T = B * S
    x2 = x.reshape(T, D)
    row = lambda v: v.reshape(1, -1)

    w_in_b = w_in[0].astype(BF16)
    w_qkv, w_uvgl = w_in_b[:, :3 * ATTN_W], w_in_b[:, 3 * ATTN_W:]
    g_mix2 = row(g_mix[0])

    qkv = _qkv_proj(x2, g_mix2, w_qkv)
    y_attn = _dilated_attention(qkv, batch=B)

    bs_b = jnp.repeat(b_sgu_spatial[0].T, SGU_GROUP_W, axis=1)
    h = _mixer(x2, y_attn, g_mix2, w_uvgl, row(b_gate[0]), w_sgu_spatial[0], bs_b,
               row(g_sgu[0]), w_branch_attn[0].astype(BF16), w_branch_sgu[0].astype(BF16),
               w_out[0].astype(BF16))

    kv = _mem_kv(mem.reshape(B * mem.shape[1], D), row(g_mem[0]), w_kv_cross[0].astype(BF16))
    h = _cross_attention(h, kv, row(g_cross[0]), w_q_cross[0].astype(BF16),
                         w_o_cross[0].astype(BF16), batch=B)

    out = _ffn(h, row(g_ffn[0]), w_gate_up[0].astype(BF16), w_down[0].astype(BF16),
               row(g_final))
    return out.reshape(B, S, D)
```

```python
import math

import jax
import jax.numpy as jnp
import numpy as np
from jax import lax
from jax.experimental import pallas as pl
from jax.experimental.pallas import tpu as pltpu

D_MODEL = 1024
HEAD_DIM = 64
DIL_GROUPS = ((128, 1), (512, 4), (2048, 16))
N_GROUPS = 3
HEADS_PER_GROUP = 4
N_ATTN_HEADS = N_GROUPS * HEADS_PER_GROUP
ATTN_W = N_ATTN_HEADS * HEAD_DIM
GROUP_W = HEADS_PER_GROUP * HEAD_DIM
BLK = 128
LANES = 128
HALVES = GROUP_W // LANES
SGU_CHUNK = 128
SGU_GROUPS = 4
SGU_W = 512
SGU_GROUP_W = SGU_W // SGU_GROUPS
N_BRANCH = 2
MEM_HEADS = 4
MEM_HEAD_DIM = 128
MEM_W = MEM_HEADS * MEM_HEAD_DIM
D_FF = 2816
EPS = 1e-6
LOG2E = math.log2(math.e)

ATTN_TILE = 2048
TM = 1024
SUB_QKV = 512
SUB_MIX = 512
SUB_CROSS = 512
SUB_FFN = 512
VMEM_LIMIT = 48 * 1024 * 1024
VMEM_LIMIT_BIG = 56 * 1024 * 1024

F32 = jnp.float32
BF16 = jnp.bfloat16


def _rms(xf, g):
    r = lax.rsqrt(jnp.mean(xf * xf, axis=-1, keepdims=True) + EPS)
    return xf * r * g


def _alibi_slopes_grouped():
    def pow2(n):
        start = 2.0 ** (-8.0 / n)
        return [start ** (i + 1) for i in range(n)]
    n = N_ATTN_HEADS
    if math.log2(n).is_integer():
        s = pow2(n)
    else:
        c = 2 ** int(math.floor(math.log2(n)))
        s = pow2(c) + pow2(2 * c)[0::2][: n - c]
    s = np.array(sorted(s, reverse=True), dtype=np.float32)
    return s.reshape(N_GROUPS, HEADS_PER_GROUP)


def _attn_bias(gi):
    window, dil = DIL_GROUPS[gi]
    n_back = window // dil
    steps = (np.arange(BLK)[:, None] + BLK) - np.arange(2 * BLK)[None, :]
    band = (steps >= 0) & (steps <= n_back)
    dist = (np.clip(steps, 0, None) * dil).astype(np.float32)
    slopes = _alibi_slopes_grouped()[gi]
    bias = -slopes[:, None, None] * dist[None]
    bias = np.where(band[None], bias.astype(np.float64) * LOG2E, -np.inf).astype(np.float32)
    return bias.reshape(HEADS_PER_GROUP * BLK, 2 * BLK)


def _const_spec(shape):
    nd = len(shape)
    return pl.BlockSpec(shape, lambda *_: (0,) * nd, pipeline_mode=pl.Buffered(1))


def _qkv_kernel(x_ref, g_ref, w_ref, *refs):
    out_refs, scr = refs[:-1], refs[-1]
    for part in range(TM // SUB_QKV):
        rows = slice(part * SUB_QKV, (part + 1) * SUB_QKV)
        a = _rms(x_ref[rows, :], g_ref[...]).astype(BF16)
        res = jnp.dot(a, w_ref[...], preferred_element_type=F32)
        for c in range(3 * ATTN_W // LANES):
            scr[part, c] = res[:, c * LANES:(c + 1) * LANES]
        for which in range(3):
            for gi, (_, dil) in enumerate(DIL_GROUPS):
                slab = (which * ATTN_W + gi * GROUP_W) // LANES
                out = out_refs[gi * 3 + which]
                out_rows = slice(part * SUB_QKV // dil, (part + 1) * SUB_QKV // dil)
                for r in range(dil):
                    for half in range(HALVES):
                        val = scr[part, slab + half, pl.ds(r, SUB_QKV // dil, stride=dil), :]
                        if which == 0:
                            val = val * (HEAD_DIM ** -0.5 * LOG2E)
                        lo = r * GROUP_W + half * LANES
                        out[out_rows, lo:lo + LANES] = val.astype(BF16)


def _qkv_proj(x2, g_mix, w_qkv):
    T = x2.shape[0]
    out_shapes, out_specs = [], []
    for _, dil in DIL_GROUPS:
        for _ in range(3):
            out_shapes.append(jax.ShapeDtypeStruct((T // dil, dil * GROUP_W), BF16))
            out_specs.append(pl.BlockSpec((TM // dil, dil * GROUP_W), lambda i: (i, 0)))
    return pl.pallas_call(
        _qkv_kernel,
        grid=(T // TM,),
        in_specs=[pl.BlockSpec((TM, D_MODEL), lambda i: (i, 0)),
                  _const_spec((1, D_MODEL)),
                  _const_spec((D_MODEL, 3 * ATTN_W))],
        out_specs=out_specs,
        out_shape=out_shapes,
        scratch_shapes=[pltpu.VMEM((TM // SUB_QKV, 3 * ATTN_W // LANES, SUB_QKV, LANES), F32)],
        compiler_params=pltpu.CompilerParams(
            dimension_semantics=("arbitrary",), vmem_limit_bytes=VMEM_LIMIT),
        name="qkv_proj",
    )(x2, g_mix, w_qkv)


def _attn_block(q, k2, v2, bias, no_prev, head_masks, low_lanes):
    zero = jnp.zeros_like(q)
    qs = jnp.concatenate([jnp.where(m, q, zero) for m in head_masks], axis=0)
    s = lax.dot_general(qs, k2, (((1,), (1,)), ((), ())),
                        preferred_element_type=F32) + bias
    if no_prev is not None:
        s = jnp.where(no_prev, -jnp.inf, s)
    mx = jnp.max(s, axis=-1, keepdims=True)
    e = jnp.exp2(s - mx)
    den = jnp.sum(e, axis=-1, keepdims=True)
    pv = jnp.dot(e.astype(BF16), v2, preferred_element_type=F32)

    def head_rows(x, h):
        return x[h * BLK:(h + 1) * BLK]

    acc, m_rep, l_rep = [], [], []
    for half in range(HALVES):
        ha, hb = 2 * half, 2 * half + 1
        cols = slice(half * LANES, (half + 1) * LANES)
        acc.append(jnp.where(low_lanes, head_rows(pv, ha)[:, cols], head_rows(pv, hb)[:, cols]))
        m_rep.append(jnp.where(low_lanes, head_rows(mx, ha), head_rows(mx, hb)))
        l_rep.append(jnp.where(low_lanes, head_rows(den, ha), head_rows(den, hb)))
    return acc, m_rep, l_rep


def _attn_kernel(*refs):
    in_refs, y_ref, nat = refs[:6 * N_GROUPS], refs[6 * N_GROUPS], refs[6 * N_GROUPS + 1]
    first_tile = pl.program_id(1) == 0
    lane = lax.broadcasted_iota(jnp.int32, (1, GROUP_W), 1)
    head_masks = [(lane >= HEAD_DIM * h) & (lane < HEAD_DIM * (h + 1))
                  for h in range(HEADS_PER_GROUP)]
    low_lanes = lax.broadcasted_iota(jnp.int32, (1, LANES), 1) < HEAD_DIM
    key_col = lax.broadcasted_iota(jnp.int32, (1, 2 * BLK), 1)
    no_prev_first = jnp.logical_and(first_tile, key_col < BLK)

    def group_blocks(gi):
        q_ref, k_ref, kp_ref, v_ref, vp_ref, bias_ref = in_refs[6 * gi:6 * gi + 6]
        dil = DIL_GROUPS[gi][1]
        for r in range(dil):
            lanes = slice(r * GROUP_W, (r + 1) * GROUP_W)
            for i in range(ATTN_TILE // dil // BLK):
                cur = slice(i * BLK, (i + 1) * BLK)
                if i == 0:
                    kp, vp = kp_ref[:, lanes], vp_ref[:, lanes]
                else:
                    prev = slice((i - 1) * BLK, i * BLK)
                    kp, vp = k_ref[prev, lanes], v_ref[prev, lanes]
                k2 = jnp.concatenate([kp, k_ref[cur, lanes]], axis=0)
                v2 = jnp.concatenate([vp, v_ref[cur, lanes]], axis=0)
                yield r, i, _attn_block(q_ref[cur, lanes], k2, v2, bias_ref[...],
                                        no_prev_first if i == 0 else None,
                                        head_masks, low_lanes)

    for gi in range(1, N_GROUPS):
        dil = DIL_GROUPS[gi][1]
        for r, i, stats in group_blocks(gi):
            dst = pl.ds(i * BLK * dil + r, BLK, stride=dil)
            for kind, parts in enumerate(stats):
                for half in range(HALVES):
                    nat[gi - 1, kind, half, dst, :] = parts[half]

    for r, i, (acc0, m0, l0) in group_blocks(0):
        rows = slice(i * BLK, (i + 1) * BLK)
        for half in range(HALVES):
            accs = [acc0[half]] + [nat[g, 0, half, rows, :] for g in range(N_GROUPS - 1)]
            ms = [m0[half]] + [nat[g, 1, half, rows, :] for g in range(N_GROUPS - 1)]
            ls = [l0[half]] + [nat[g, 2, half, rows, :] for g in range(N_GROUPS - 1)]
            m_all = jnp.maximum(jnp.maximum(ms[0], ms[1]), ms[2])
            ws = [jnp.exp2(m - m_all) for m in ms]
            num = ws[0] * accs[0] + ws[1] * accs[1] + ws[2] * accs[2]
            den = ws[0] * ls[0] + ws[1] * ls[1] + ws[2] * ls[2]
            y_ref[rows, half * LANES:(half + 1) * LANES] = (num * (1.0 / den)).astype(BF16)


def _dilated_attention(qkv, batch):
    T = qkv[0].shape[0]
    tiles_per_batch = T // ATTN_TILE // batch

    def cur_map(b, t):
        return (b * tiles_per_batch + t, 0)

    operands, in_specs = [], []
    for gi, (_, dil) in enumerate(DIL_GROUPS):
        qd, kd, vd = qkv[3 * gi:3 * gi + 3]
        rows, width = ATTN_TILE // dil, dil * GROUP_W
        prev_blocks = rows // BLK

        def prev_map(b, t, prev_blocks=prev_blocks):
            return (jnp.maximum((b * tiles_per_batch + t) * prev_blocks - 1, 0), 0)

        bias = jnp.asarray(_attn_bias(gi))
        operands += [qd, kd, kd, vd, vd, bias]
        in_specs += [pl.BlockSpec((rows, width), cur_map),
                     pl.BlockSpec((rows, width), cur_map),
                     pl.BlockSpec((BLK, width), prev_map),
                     pl.BlockSpec((rows, width), cur_map),
                     pl.BlockSpec((BLK, width), prev_map),
                     _const_spec(bias.shape)]
    return pl.pallas_call(
        _attn_kernel,
        grid=(batch, tiles_per_batch),
        in_specs=in_specs,
        out_specs=pl.BlockSpec((ATTN_TILE, GROUP_W), cur_map),
        out_shape=jax.ShapeDtypeStruct((T, GROUP_W), BF16),
        scratch_shapes=[pltpu.VMEM((N_GROUPS - 1, 3, HALVES, ATTN_TILE, LANES), F32)],
        compiler_params=pltpu.CompilerParams(
            dimension_semantics=("arbitrary", "arbitrary"), vmem_limit_bytes=VMEM_LIMIT),
        name="dilated_attn",
    )(*operands)


def _mixer_kernel(x_ref, y_ref, g_ref, w_ref, bg_ref, ws_ref, bs_ref, gs_ref,
                  wba_ref, wbs_ref, wo_ref, h_ref):
    row = lax.broadcasted_iota(jnp.int32, (SGU_CHUNK, SGU_CHUNK), 0)
    col = lax.broadcasted_iota(jnp.int32, (SGU_CHUNK, SGU_CHUNK), 1)
    ws = [jnp.where(row >= col, ws_ref[g], 0.0).astype(BF16) for g in range(SGU_GROUPS)]
    for part in range(TM // SUB_MIX):
        rows = slice(part * SUB_MIX, (part + 1) * SUB_MIX)
        x = x_ref[rows, :]
        a = _rms(x, g_ref[...]).astype(BF16)
        proj = jnp.dot(a, w_ref[...], preferred_element_type=F32)

        z = jax.nn.gelu(proj[:, :2 * SGU_W], approximate=True)
        u = z[:, :SGU_W]
        v = _rms(z[:, SGU_W:], gs_ref[...]).astype(BF16)
        chunks = []
        for c in range(SUB_MIX // SGU_CHUNK):
            rs = slice(c * SGU_CHUNK, (c + 1) * SGU_CHUNK)
            parts = [jnp.dot(ws[g], v[rs, g * SGU_GROUP_W:(g + 1) * SGU_GROUP_W],
                             preferred_element_type=F32) for g in range(SGU_GROUPS)]
            chunks.append(jnp.concatenate(parts, axis=1) + bs_ref[...])
        y_sgu = u * jnp.concatenate(chunks, axis=0)

        gates = 1.0 / (1.0 + jnp.exp(-(proj[:, 2 * SGU_W:] + bg_ref[...])))
        merged = (gates[:, :D_MODEL] * jnp.dot(y_ref[rows, :], wba_ref[...],
                                                preferred_element_type=F32)
                  + gates[:, D_MODEL:] * jnp.dot(y_sgu.astype(BF16), wbs_ref[...],
                                                 preferred_element_type=F32))
        h_ref[rows, :] = x + jnp.dot(merged.astype(BF16), wo_ref[...],
                                     preferred_element_type=F32)


def _mixer(x2, y_attn, g_mix, w_uvgl, b_gate, w_s, bs_b, g_sgu, w_ba, w_bs, w_out):
    T = x2.shape[0]
    row_spec = lambda w: pl.BlockSpec((TM, w), lambda i: (i, 0))
    consts = (g_mix, w_uvgl, b_gate, w_s, bs_b, g_sgu, w_ba, w_bs, w_out)
    return pl.pallas_call(
        _mixer_kernel,
        grid=(T // TM,),
        in_specs=([row_spec(D_MODEL), row_spec(GROUP_W)]
                  + [_const_spec(c.shape) for c in consts]),
        out_specs=row_spec(D_MODEL),
        out_shape=jax.ShapeDtypeStruct((T, D_MODEL), F32),
        compiler_params=pltpu.CompilerParams(
            dimension_semantics=("arbitrary",), vmem_limit_bytes=VMEM_LIMIT),
        name="mixer_tail",
    )(x2, y_attn, *consts)


def _mem_kv_kernel(m_ref, g_ref, w_ref, kv_ref):
    m = _rms(m_ref[...], g_ref[...]).astype(BF16)
    kv_ref[...] = jnp.dot(m, w_ref[...], preferred_element_type=F32).astype(BF16)


def _mem_kv(mem2, g_mem, w_kv):
    rows = mem2.shape[0]
    return pl.pallas_call(
        _mem_kv_kernel,
        grid=(1,),
        in_specs=[_const_spec(mem2.shape), _const_spec(g_mem.shape), _const_spec(w_kv.shape)],
        out_specs=pl.BlockSpec((rows, 2 * MEM_W), lambda i: (0, 0)),
        out_shape=jax.ShapeDtypeStruct((rows, 2 * MEM_W), BF16),
        compiler_params=pltpu.CompilerParams(
            dimension_semantics=("arbitrary",), vmem_limit_bytes=VMEM_LIMIT),
        name="mem_kv",
    )(mem2, g_mem, w_kv)


def _cross_kernel(h_ref, kv_ref, g_ref, wq_ref, wo_ref, out_ref):
    for part in range(TM // SUB_CROSS):
        rows = slice(part * SUB_CROSS, (part + 1) * SUB_CROSS)
        h = h_ref[rows, :]
        c = _rms(h, g_ref[...]).astype(BF16)
        q = jnp.dot(c, wq_ref[...], preferred_element_type=F32).astype(BF16)
        heads = []
        for hd in range(MEM_HEADS):
            cols = slice(hd * MEM_HEAD_DIM, (hd + 1) * MEM_HEAD_DIM)
            k = kv_ref[:, cols]
            v = kv_ref[:, MEM_W + hd * MEM_HEAD_DIM:MEM_W + (hd + 1) * MEM_HEAD_DIM]
            s = lax.dot_general(q[:, cols], k, (((1,), (1,)), ((), ())),
                                preferred_element_type=F32) * (MEM_HEAD_DIM ** -0.5)
            e = jnp.exp(s - jnp.max(s, axis=-1, keepdims=True))
            den = jnp.sum(e, axis=-1, keepdims=True)
            o = jnp.dot(e.astype(BF16), v, preferred_element_type=F32) * (1.0 / den)
            heads.append(o.astype(BF16))
        o = jnp.concatenate(heads, axis=1)
        out_ref[rows, :] = h + jnp.dot(o, wo_ref[...], preferred_element_type=F32)


def _cross_attention(h, kv, g_cross, w_q, w_o, batch):
    T = h.shape[0]
    blocks_per_batch = T // batch // TM
    mem_len = kv.shape[0] // batch
    return pl.pallas_call(
        _cross_kernel,
        grid=(batch, blocks_per_batch),
        in_specs=[pl.BlockSpec((TM, D_MODEL), lambda b, i: (b * blocks_per_batch + i, 0)),
                  pl.BlockSpec((mem_len, 2 * MEM_W), lambda b, i: (b, 0)),
                  _const_spec(g_cross.shape), _const_spec(w_q.shape), _const_spec(w_o.shape)],
        out_specs=pl.BlockSpec((TM, D_MODEL), lambda b, i: (b * blocks_per_batch + i, 0)),
        out_shape=jax.ShapeDtypeStruct((T, D_MODEL), F32),
        compiler_params=pltpu.CompilerParams(
            dimension_semantics=("arbitrary", "arbitrary"), vmem_limit_bytes=VMEM_LIMIT),
        name="cross_attn",
    )(h, kv, g_cross, w_q, w_o)


def _ffn_kernel(h_ref, g_ref, wgu_ref, wd_ref, gf_ref, out_ref):
    for part in range(TM // SUB_FFN):
        rows = slice(part * SUB_FFN, (part + 1) * SUB_FFN)
        h = h_ref[rows, :]
        f = _rms(h, g_ref[...]).astype(BF16)
        gu = jnp.dot(f, wgu_ref[...], preferred_element_type=F32)
        gt, up = gu[:, :D_FF], gu[:, D_FF:]
        act = (gt * (1.0 / (1.0 + jnp.exp(-gt))) * up).astype(BF16)
        acc = h + jnp.dot(act, wd_ref[...], preferred_element_type=F32)
        out_ref[rows, :] = _rms(acc, gf_ref[...])


def _ffn(h, g_ffn, w_gu, w_down, g_final):
    T = h.shape[0]
    consts = (g_ffn, w_gu, w_down, g_final)
    return pl.pallas_call(
        _ffn_kernel,
        grid=(T // TM,),
        in_specs=[pl.BlockSpec((TM, D_MODEL), lambda i: (i, 0))]
                 + [_const_spec(c.shape) for c in consts],
        out_specs=pl.BlockSpec((TM, D_MODEL), lambda i: (i, 0)),
        out_shape=jax.ShapeDtypeStruct((T, D_MODEL), F32),
        compiler_params=pltpu.CompilerParams(
            dimension_semantics=("arbitrary",), vmem_limit_bytes=VMEM_LIMIT_BIG),
        name="ffn",
    )(h, *consts)


def kernel(x, mem, g_mix, w_in, b_gate, w_sgu_spatial, b_sgu_spatial, g_sgu, w_branch_attn,
           w_branch_sgu, w_out, g_cross, g_mem, w_q_cross, w_kv_cross, w_o_cross, g_ffn,
           w_gate_up, w_down, g_final):
    B, S, D = x.shape
    assert D == D_MODEL and S % ATTN_TILE == 0 and w_in.shape[0] == 1
    T = B * S
    x2 = x.reshape(T, D)
    row = lambda v: v.reshape(1, -1)

    w_in_b = w_in[0].astype(BF16)
    w_qkv, w_uvgl = w_in_b[:, :3 * ATTN_W], w_in_b[:, 3 * ATTN_W:]
    g_mix2 = row(g_mix[0])

    qkv = _qkv_proj(x2, g_mix2, w_qkv)
    y_attn = _dilated_attention(qkv, batch=B)

    bs_b = jnp.repeat(b_sgu_spatial[0].T, SGU_GROUP_W, axis=1)
    h = _mixer(x2, y_attn, g_mix2, w_uvgl, row(b_gate[0]), w_sgu_spatial[0], bs_b,
               row(g_sgu[0]), w_branch_attn[0].astype(BF16), w_branch_sgu[0].astype(BF16),
               w_out[0].astype(BF16))

    kv = _mem_kv(mem.reshape(B * mem.shape[1], D), row(g_mem[0]), w_kv_cross[0].astype(BF16))
    h = _cross_attention(h, kv, row(g_cross[0]), w_q_cross[0].astype(BF16),
                         w_o_cross[0].astype(BF16), batch=B)

    out = _ffn(h, row(g_ffn[0]), w_gate_up[0].astype(BF16), w_down[0].astype(BF16),
               row(g_final))
    return out.reshape(B, S, D)
```

```python
import functools
import math

import jax
import jax.numpy as jnp
import numpy as np
from jax import lax
from jax.experimental import pallas as pl
from jax.experimental.pallas import tpu as pltpu

D_MODEL = 1024
HEAD_DIM = 64
DIL_GROUPS = ((128, 1), (512, 4), (2048, 16))
N_GROUPS = 3
HEADS_PER_GROUP = 4
N_ATTN_HEADS = N_GROUPS * HEADS_PER_GROUP
ATTN_W = N_ATTN_HEADS * HEAD_DIM
GROUP_W = HEADS_PER_GROUP * HEAD_DIM
BLK = 128
LANES = 128
BF16_SUBLANES = 16
HALVES = GROUP_W // LANES
SGU_CHUNK = 128
SGU_GROUPS = 4
SGU_W = 512
SGU_GROUP_W = SGU_W // SGU_GROUPS
N_BRANCH = 2
MEM_HEADS = 4
MEM_HEAD_DIM = 128
MEM_W = MEM_HEADS * MEM_HEAD_DIM
D_FF = 2816
EPS = 1e-6
LOG2E = math.log2(math.e)

ATTN_TILE = 2048
TM = 1024
SUB_QKV = 512
SUB_MIX = 512
SUB_CROSS = 512
SUB_FFN = 512
VMEM_LIMIT = 48 * 1024 * 1024
VMEM_LIMIT_BIG = 56 * 1024 * 1024

F32 = jnp.float32
BF16 = jnp.bfloat16


def _rms(xf, g):
    r = lax.rsqrt(jnp.mean(xf * xf, axis=-1, keepdims=True) + EPS)
    return xf * r * g


def _alibi_slopes_grouped():
    def pow2(n):
        start = 2.0 ** (-8.0 / n)
        return [start ** (i + 1) for i in range(n)]
    n = N_ATTN_HEADS
    if math.log2(n).is_integer():
        s = pow2(n)
    else:
        c = 2 ** int(math.floor(math.log2(n)))
        s = pow2(c) + pow2(2 * c)[0::2][: n - c]
    s = np.array(sorted(s, reverse=True), dtype=np.float32)
    return s.reshape(N_GROUPS, HEADS_PER_GROUP)


def _attn_bias(gi):
    window, dil = DIL_GROUPS[gi]
    n_back = window // dil
    steps = (np.arange(BLK)[:, None] + BLK) - np.arange(2 * BLK)[None, :]
    band = (steps >= 0) & (steps <= n_back)
    dist = (np.clip(steps, 0, None) * dil).astype(np.float32)
    slopes = _alibi_slopes_grouped()[gi]
    bias = -slopes[:, None, None] * dist[None]
    bias = np.where(band[None], bias.astype(np.float64) * LOG2E, -np.inf).astype(np.float32)
    return bias.reshape(HEADS_PER_GROUP * BLK, 2 * BLK)


def _const_spec(shape):
    nd = len(shape)
    return pl.BlockSpec(shape, lambda *_: (0,) * nd, pipeline_mode=pl.Buffered(1))


def _cast_specs(weights, steps):
    in_specs, out_specs, out_shapes = [], [], []
    for w in weights:
        rows, cols = w.shape
        assert rows % (steps * BF16_SUBLANES) == 0
        in_specs.append(pl.BlockSpec((rows // steps, cols), lambda i, *_: (i, 0)))
        out_specs.append(pl.BlockSpec((rows // steps, cols), lambda i, *_: (i, 0)))
        out_shapes.append(jax.ShapeDtypeStruct(w.shape, BF16))
    return in_specs, out_specs, out_shapes


def _cast_slabs(in_refs, out_refs):
    for src, dst in zip(in_refs, out_refs):
        dst[...] = src[...].astype(BF16)


def _qkv_kernel(x_ref, g_ref, wf_ref, *refs, n_cast):
    cast_in, refs = refs[:n_cast], refs[n_cast:]
    out_refs, cast_out = refs[:3 * N_GROUPS], refs[3 * N_GROUPS:3 * N_GROUPS + n_cast]
    w_ref, scr = refs[-2:]

    @pl.when(pl.program_id(0) == 0)
    def _():
        w_ref[...] = wf_ref[...].astype(BF16)

    _cast_slabs(cast_in, cast_out)
    for part in range(TM // SUB_QKV):
        rows = slice(part * SUB_QKV, (part + 1) * SUB_QKV)
        a = _rms(x_ref[rows, :], g_ref[...]).astype(BF16)
        res = jnp.dot(a, w_ref[...], preferred_element_type=F32)
        for c in range(3 * ATTN_W // LANES):
            scr[part, c] = res[:, c * LANES:(c + 1) * LANES]
        for which in range(3):
            for gi, (_, dil) in enumerate(DIL_GROUPS):
                slab = (which * ATTN_W + gi * GROUP_W) // LANES
                out = out_refs[gi * 3 + which]
                out_rows = slice(part * SUB_QKV // dil, (part + 1) * SUB_QKV // dil)
                for r in range(dil):
                    for half in range(HALVES):
                        val = scr[part, slab + half, pl.ds(r, SUB_QKV // dil, stride=dil), :]
                        if which == 0:
                            val = val * (HEAD_DIM ** -0.5 * LOG2E)
                        lo = r * GROUP_W + half * LANES
                        out[out_rows, lo:lo + LANES] = val.astype(BF16)


def _qkv_proj(x2, g_mix, w_in, cast_weights):
    T = x2.shape[0]
    steps = T // TM
    out_shapes, out_specs = [], []
    for _, dil in DIL_GROUPS:
        for _ in range(3):
            out_shapes.append(jax.ShapeDtypeStruct((T // dil, dil * GROUP_W), BF16))
            out_specs.append(pl.BlockSpec((TM // dil, dil * GROUP_W), lambda i: (i, 0)))
    cast_in, cast_out, cast_shapes = _cast_specs(cast_weights, steps)
    return pl.pallas_call(
        functools.partial(_qkv_kernel, n_cast=len(cast_weights)),
        grid=(steps,),
        in_specs=[pl.BlockSpec((TM, D_MODEL), lambda i: (i, 0)),
                  _const_spec((1, D_MODEL)),
                  _const_spec((D_MODEL, 3 * ATTN_W))] + cast_in,
        out_specs=out_specs + cast_out,
        out_shape=out_shapes + cast_shapes,
        scratch_shapes=[pltpu.VMEM((D_MODEL, 3 * ATTN_W), BF16),
                        pltpu.VMEM((TM // SUB_QKV, 3 * ATTN_W // LANES, SUB_QKV, LANES), F32)],
        compiler_params=pltpu.CompilerParams(
            dimension_semantics=("arbitrary",), vmem_limit_bytes=VMEM_LIMIT),
        name="qkv_proj",
    )(x2, g_mix, w_in, *cast_weights)


def _attn_block(q, k2, v2, bias, no_prev, head_masks, low_lanes):
    zero = jnp.zeros_like(q)
    qs = jnp.concatenate([jnp.where(m, q, zero) for m in head_masks], axis=0)
    s = lax.dot_general(qs, k2, (((1,), (1,)), ((), ())),
                        preferred_element_type=F32) + bias
    if no_prev is not None:
        s = jnp.where(no_prev, -jnp.inf, s)
    mx = jnp.max(s, axis=-1, keepdims=True)
    e = jnp.exp2(s - mx)
    den = jnp.sum(e, axis=-1, keepdims=True)
    pv = jnp.dot(e.astype(BF16), v2, preferred_element_type=F32)

    def head_rows(x, h):
        return x[h * BLK:(h + 1) * BLK]

    acc, m_rep, l_rep = [], [], []
    for half in range(HALVES):
        ha, hb = 2 * half, 2 * half + 1
        cols = slice(half * LANES, (half + 1) * LANES)
        acc.append(jnp.where(low_lanes, head_rows(pv, ha)[:, cols], head_rows(pv, hb)[:, cols]))
        m_rep.append(jnp.where(low_lanes, head_rows(mx, ha), head_rows(mx, hb)))
        l_rep.append(jnp.where(low_lanes, head_rows(den, ha), head_rows(den, hb)))
    return acc, m_rep, l_rep


def _attn_kernel(*refs):
    in_refs, y_ref, nat = refs[:6 * N_GROUPS], refs[6 * N_GROUPS], refs[6 * N_GROUPS + 1]
    first_tile = pl.program_id(1) == 0
    lane = lax.broadcasted_iota(jnp.int32, (1, GROUP_W), 1)
    head_masks = [(lane >= HEAD_DIM * h) & (lane < HEAD_DIM * (h + 1))
                  for h in range(HEADS_PER_GROUP)]
    low_lanes = lax.broadcasted_iota(jnp.int32, (1, LANES), 1) < HEAD_DIM
    key_col = lax.broadcasted_iota(jnp.int32, (1, 2 * BLK), 1)
    no_prev_first = jnp.logical_and(first_tile, key_col < BLK)

    def group_blocks(gi):
        q_ref, k_ref, kp_ref, v_ref, vp_ref, bias_ref = in_refs[6 * gi:6 * gi + 6]
        dil = DIL_GROUPS[gi][1]
        for r in range(dil):
            lanes = slice(r * GROUP_W, (r + 1) * GROUP_W)
            for i in range(ATTN_TILE // dil // BLK):
                cur = slice(i * BLK, (i + 1) * BLK)
                if i == 0:
                    kp, vp = kp_ref[:, lanes], vp_ref[:, lanes]
                else:
                    prev = slice((i - 1) * BLK, i * BLK)
                    kp, vp = k_ref[prev, lanes], v_ref[prev, lanes]
                k2 = jnp.concatenate([kp, k_ref[cur, lanes]], axis=0)
                v2 = jnp.concatenate([vp, v_ref[cur, lanes]], axis=0)
                yield r, i, _attn_block(q_ref[cur, lanes], k2, v2, bias_ref[...],
                                        no_prev_first if i == 0 else None,
                                        head_masks, low_lanes)

    for gi in range(1, N_GROUPS):
        dil = DIL_GROUPS[gi][1]
        for r, i, stats in group_blocks(gi):
            dst = pl.ds(i * BLK * dil + r, BLK, stride=dil)
            for kind, parts in enumerate(stats):
                for half in range(HALVES):
                    nat[gi - 1, kind, half, dst, :] = parts[half]

    for r, i, (acc0, m0, l0) in group_blocks(0):
        rows = slice(i * BLK, (i + 1) * BLK)
        for half in range(HALVES):
            accs = [acc0[half]] + [nat[g, 0, half, rows, :] for g in range(N_GROUPS - 1)]
            ms = [m0[half]] + [nat[g, 1, half, rows, :] for g in range(N_GROUPS - 1)]
            ls = [l0[half]] + [nat[g, 2, half, rows, :] for g in range(N_GROUPS - 1)]
            m_all = jnp.maximum(jnp.maximum(ms[0], ms[1]), ms[2])
            ws = [jnp.exp2(m - m_all) for m in ms]
            num = ws[0] * accs[0] + ws[1] * accs[1] + ws[2] * accs[2]
            den = ws[0] * ls[0] + ws[1] * ls[1] + ws[2] * ls[2]
            y_ref[rows, half * LANES:(half + 1) * LANES] = (num * (1.0 / den)).astype(BF16)


def _dilated_attention(qkv, batch):
    T = qkv[0].shape[0]
    tiles_per_batch = T // ATTN_TILE // batch

    def cur_map(b, t):
        return (b * tiles_per_batch + t, 0)

    operands, in_specs = [], []
    for gi, (_, dil) in enumerate(DIL_GROUPS):
        qd, kd, vd = qkv[3 * gi:3 * gi + 3]
        rows, width = ATTN_TILE // dil, dil * GROUP_W
        prev_blocks = rows // BLK

        def prev_map(b, t, prev_blocks=prev_blocks):
            return (jnp.maximum((b * tiles_per_batch + t) * prev_blocks - 1, 0), 0)

        bias = jnp.asarray(_attn_bias(gi))
        operands += [qd, kd, kd, vd, vd, bias]
        in_specs += [pl.BlockSpec((rows, width), cur_map),
                     pl.BlockSpec((rows, width), cur_map),
                     pl.BlockSpec((BLK, width), prev_map),
                     pl.BlockSpec((rows, width), cur_map),
                     pl.BlockSpec((BLK, width), prev_map),
                     _const_spec(bias.shape)]
    return pl.pallas_call(
        _attn_kernel,
        grid=(batch, tiles_per_batch),
        in_specs=in_specs,
        out_specs=pl.BlockSpec((ATTN_TILE, GROUP_W), cur_map),
        out_shape=jax.ShapeDtypeStruct((T, GROUP_W), BF16),
        scratch_shapes=[pltpu.VMEM((N_GROUPS - 1, 3, HALVES, ATTN_TILE, LANES), F32)],
        compiler_params=pltpu.CompilerParams(
            dimension_semantics=("arbitrary", "arbitrary"), vmem_limit_bytes=VMEM_LIMIT),
        name="dilated_attn",
    )(*operands)


def _mixer_kernel(x_ref, y_ref, g_ref, win_ref, bg_ref, ws_ref, bs_ref, gs_ref,
                  wba_ref, wbs_ref, wo_ref, *refs, n_cast):
    cast_in, h_ref, cast_out = refs[:n_cast], refs[n_cast], refs[n_cast + 1:]
    _cast_slabs(cast_in, cast_out)
    row = lax.broadcasted_iota(jnp.int32, (SGU_CHUNK, SGU_CHUNK), 0)
    col = lax.broadcasted_iota(jnp.int32, (SGU_CHUNK, SGU_CHUNK), 1)
    ws = [jnp.where(row >= col, ws_ref[g], 0.0).astype(BF16) for g in range(SGU_GROUPS)]
    for part in range(TM // SUB_MIX):
        rows = slice(part * SUB_MIX, (part + 1) * SUB_MIX)
        x = x_ref[rows, :]
        a = _rms(x, g_ref[...]).astype(BF16)
        proj = jnp.dot(a, win_ref[:, 3 * ATTN_W:],
                       preferred_element_type=F32)

        z = jax.nn.gelu(proj[:, :2 * SGU_W], approximate=True)
        u = z[:, :SGU_W]
        v = _rms(z[:, SGU_W:], gs_ref[...]).astype(BF16)
        chunks = []
        for c in range(SUB_MIX // SGU_CHUNK):
            rs = slice(c * SGU_CHUNK, (c + 1) * SGU_CHUNK)
            parts = [jnp.dot(ws[g], v[rs, g * SGU_GROUP_W:(g + 1) * SGU_GROUP_W],
                             preferred_element_type=F32) for g in range(SGU_GROUPS)]
            chunks.append(jnp.concatenate(parts, axis=1) + bs_ref[...])
        y_sgu = u * jnp.concatenate(chunks, axis=0)

        gates = 1.0 / (1.0 + jnp.exp(-(proj[:, 2 * SGU_W:] + bg_ref[...])))
        merged = (gates[:, :D_MODEL] * jnp.dot(y_ref[rows, :], wba_ref[...],
                                                preferred_element_type=F32)
                  + gates[:, D_MODEL:] * jnp.dot(y_sgu.astype(BF16), wbs_ref[...],
                                                 preferred_element_type=F32))
        h_ref[rows, :] = x + jnp.dot(merged.astype(BF16), wo_ref[...],
                                     preferred_element_type=F32)


def _mixer(x2, y_attn, g_mix, w_in_b, b_gate, w_s, bs_b, g_sgu, w_ba, w_bs, w_out, cast_weights):
    T = x2.shape[0]
    steps = T // TM
    row_spec = lambda w: pl.BlockSpec((TM, w), lambda i: (i, 0))
    consts = (g_mix, w_in_b, b_gate, w_s, bs_b, g_sgu, w_ba, w_bs, w_out)
    cast_in, cast_out, cast_shapes = _cast_specs(cast_weights, steps)
    return pl.pallas_call(
        functools.partial(_mixer_kernel, n_cast=len(cast_weights)),
        grid=(steps,),
        in_specs=([row_spec(D_MODEL), row_spec(GROUP_W)]
                  + [_const_spec(c.shape) for c in consts] + cast_in),
        out_specs=[row_spec(D_MODEL)] + cast_out,
        out_shape=[jax.ShapeDtypeStruct((T, D_MODEL), F32)] + cast_shapes,
        compiler_params=pltpu.CompilerParams(
            dimension_semantics=("arbitrary",), vmem_limit_bytes=VMEM_LIMIT_BIG),
        name="mixer_tail",
    )(x2, y_attn, *consts, *cast_weights)


def _mem_kv_kernel(m_ref, g_ref, w_ref, kv_ref):
    m = _rms(m_ref[...], g_ref[...]).astype(BF16)
    kv_ref[...] = jnp.dot(m, w_ref[...], preferred_element_type=F32).astype(BF16)


def _mem_kv(mem2, g_mem, w_kv):
    rows = mem2.shape[0]
    return pl.pallas_call(
        _mem_kv_kernel,
        grid=(1,),
        in_specs=[_const_spec(mem2.shape), _const_spec(g_mem.shape), _const_spec(w_kv.shape)],
        out_specs=pl.BlockSpec((rows, 2 * MEM_W), lambda i: (0, 0)),
        out_shape=jax.ShapeDtypeStruct((rows, 2 * MEM_W), BF16),
        compiler_params=pltpu.CompilerParams(
            dimension_semantics=("arbitrary",), vmem_limit_bytes=VMEM_LIMIT),
        name="mem_kv",
    )(mem2, g_mem, w_kv)


def _cross_kernel(h_ref, kv_ref, g_ref, wq_ref, wo_ref, out_ref):
    for part in range(TM // SUB_CROSS):
        rows = slice(part * SUB_CROSS, (part + 1) * SUB_CROSS)
        h = h_ref[rows, :]
        c = _rms(h, g_ref[...]).astype(BF16)
        q = jnp.dot(c, wq_ref[...], preferred_element_type=F32).astype(BF16)
        heads = []
        for hd in range(MEM_HEADS):
            cols = slice(hd * MEM_HEAD_DIM, (hd + 1) * MEM_HEAD_DIM)
            k = kv_ref[:, cols]
            v = kv_ref[:, MEM_W + hd * MEM_HEAD_DIM:MEM_W + (hd + 1) * MEM_HEAD_DIM]
            s = lax.dot_general(q[:, cols], k, (((1,), (1,)), ((), ())),
                                preferred_element_type=F32) * (MEM_HEAD_DIM ** -0.5)
            e = jnp.exp(s - jnp.max(s, axis=-1, keepdims=True))
            den = jnp.sum(e, axis=-1, keepdims=True)
            o = jnp.dot(e.astype(BF16), v, preferred_element_type=F32) * (1.0 / den)
            heads.append(o.astype(BF16))
        o = jnp.concatenate(heads, axis=1)
        out_ref[rows, :] = h + jnp.dot(o, wo_ref[...], preferred_element_type=F32)


def _cross_attention(h, kv, g_cross, w_q, w_o, batch):
    T = h.shape[0]
    blocks_per_batch = T // batch // TM
    mem_len = kv.shape[0] // batch
    return pl.pallas_call(
        _cross_kernel,
        grid=(batch, blocks_per_batch),
        in_specs=[pl.BlockSpec((TM, D_MODEL), lambda b, i: (b * blocks_per_batch + i, 0)),
                  pl.BlockSpec((mem_len, 2 * MEM_W), lambda b, i: (b, 0)),
                  _const_spec(g_cross.shape), _const_spec(w_q.shape), _const_spec(w_o.shape)],
        out_specs=pl.BlockSpec((TM, D_MODEL), lambda b, i: (b * blocks_per_batch + i, 0)),
        out_shape=jax.ShapeDtypeStruct((T, D_MODEL), F32),
        compiler_params=pltpu.CompilerParams(
            dimension_semantics=("arbitrary", "arbitrary"), vmem_limit_bytes=VMEM_LIMIT),
        name="cross_attn",
    )(h, kv, g_cross, w_q, w_o)


def _ffn_kernel(h_ref, g_ref, wgu_ref, wd_ref, gf_ref, out_ref):
    for part in range(TM // SUB_FFN):
        rows = slice(part * SUB_FFN, (part + 1) * SUB_FFN)
        h = h_ref[rows, :]
        f = _rms(h, g_ref[...]).astype(BF16)
        gu = jnp.dot(f, wgu_ref[...], preferred_element_type=F32)
        gt, up = gu[:, :D_FF], gu[:, D_FF:]
        act = (gt * (1.0 / (1.0 + jnp.exp(-gt))) * up).astype(BF16)
        acc = h + jnp.dot(act, wd_ref[...], preferred_element_type=F32)
        out_ref[rows, :] = _rms(acc, gf_ref[...])


def _ffn(h, g_ffn, w_gu, w_down, g_final):
    T = h.shape[0]
    consts = (g_ffn, w_gu, w_down, g_final)
    return pl.pallas_call(
        _ffn_kernel,
        grid=(T // TM,),
        in_specs=[pl.BlockSpec((TM, D_MODEL), lambda i: (i, 0))]
                 + [_const_spec(c.shape) for c in consts],
        out_specs=pl.BlockSpec((TM, D_MODEL), lambda i: (i, 0)),
        out_shape=jax.ShapeDtypeStruct((T, D_MODEL), F32),
        compiler_params=pltpu.CompilerParams(
            dimension_semantics=("arbitrary",), vmem_limit_bytes=VMEM_LIMIT_BIG),
        name="ffn",
    )(h, *consts)


def kernel(x, mem, g_mix, w_in, b_gate, w_sgu_spatial, b_sgu_spatial, g_sgu, w_branch_attn,
           w_branch_sgu, w_out, g_cross, g_mem, w_q_cross, w_kv_cross, w_o_cross, g_ffn,
           w_gate_up, w_down, g_final):
    B, S, D = x.shape
    assert D == D_MODEL and S % ATTN_TILE == 0 and w_in.shape[0] == 1
    T = B * S
    x2 = x.reshape(T, D)
    row = lambda v: v.reshape(1, -1)

    g_mix2 = row(g_mix[0])
    w_in2 = w_in.reshape(D, w_in.shape[-1])

    *qkv, w_in_b, w_ba, w_bs, w_o1 = _qkv_proj(
        x2, g_mix2, w_in2, (w_in2, w_branch_attn[0], w_branch_sgu[0], w_out[0]))
    y_attn = _dilated_attention(qkv, batch=B)

    bs_b = jnp.repeat(b_sgu_spatial[0].T, SGU_GROUP_W, axis=1)
    h, w_q, w_kv, w_o2, w_gu, w_dn = _mixer(
        x2, y_attn, g_mix2, w_in_b, row(b_gate[0]), w_sgu_spatial[0], bs_b, row(g_sgu[0]),
        w_ba, w_bs, w_o1,
        (w_q_cross[0], w_kv_cross[0], w_o_cross[0], w_gate_up[0], w_down[0]))

    kv = _mem_kv(mem.reshape(B * mem.shape[1], D), row(g_mem[0]), w_kv)
    h = _cross_attention(h, kv, row(g_cross[0]), w_q, w_o2, batch=B)
    out = _ffn(h, row(g_ffn[0]), w_gu, w_dn, row(g_final))
    return out.reshape(B, S, D)
```

```python
import functools
import math

import jax
import jax.numpy as jnp
import numpy as np
from jax import lax
from jax.experimental import pallas as pl
from jax.experimental.pallas import tpu as pltpu

D_MODEL = 1024
HEAD_DIM = 64
DIL_GROUPS = ((128, 1), (512, 4), (2048, 16))
N_GROUPS = 3
HEADS_PER_GROUP = 4
N_ATTN_HEADS = N_GROUPS * HEADS_PER_GROUP
ATTN_W = N_ATTN_HEADS * HEAD_DIM
GROUP_W = HEADS_PER_GROUP * HEAD_DIM
BLK = 128
LANES = 128
BF16_SUBLANES = 16
MAX_SUBLANE_STRIDE = 4
HALVES = GROUP_W // LANES
SGU_CHUNK = 128
SGU_GROUPS = 4
SGU_W = 512
SGU_GROUP_W = SGU_W // SGU_GROUPS
N_BRANCH = 2
MEM_HEADS = 4
MEM_HEAD_DIM = 128
MEM_W = MEM_HEADS * MEM_HEAD_DIM
D_FF = 2816
EPS = 1e-6
LOG2E = math.log2(math.e)

ATTN_TILE = 2048
TM = 1024
SUB_QKV = 512
SUB_MIX = 512
SUB_CROSS = 512
SUB_FFN = 512
VMEM_LIMIT = 48 * 1024 * 1024
VMEM_LIMIT_BIG = 56 * 1024 * 1024

F32 = jnp.float32
BF16 = jnp.bfloat16


def _rms(xf, g):
    r = lax.rsqrt(jnp.mean(xf * xf, axis=-1, keepdims=True) + EPS)
    return xf * r * g


def _alibi_slopes_grouped():
    def pow2(n):
        start = 2.0 ** (-8.0 / n)
        return [start ** (i + 1) for i in range(n)]
    n = N_ATTN_HEADS
    if math.log2(n).is_integer():
        s = pow2(n)
    else:
        c = 2 ** int(math.floor(math.log2(n)))
        s = pow2(c) + pow2(2 * c)[0::2][: n - c]
    s = np.array(sorted(s, reverse=True), dtype=np.float32)
    return s.reshape(N_GROUPS, HEADS_PER_GROUP)


def _attn_bias(gi):
    window, dil = DIL_GROUPS[gi]
    n_back = window // dil
    steps = (np.arange(BLK)[:, None] + BLK) - np.arange(2 * BLK)[None, :]
    band = (steps >= 0) & (steps <= n_back)
    dist = (np.clip(steps, 0, None) * dil).astype(np.float32)
    slopes = _alibi_slopes_grouped()[gi]
    bias = -slopes[:, None, None] * dist[None]
    bias = np.where(band[None], bias.astype(np.float64) * LOG2E, -np.inf).astype(np.float32)
    return bias.reshape(HEADS_PER_GROUP * BLK, 2 * BLK)


def _const_spec(shape):
    nd = len(shape)
    return pl.BlockSpec(shape, lambda *_: (0,) * nd, pipeline_mode=pl.Buffered(1))


def _cast_specs(weights, steps):
    in_specs, out_specs, out_shapes = [], [], []
    for w in weights:
        rows, cols = w.shape
        assert rows % (steps * BF16_SUBLANES) == 0
        in_specs.append(pl.BlockSpec((rows // steps, cols), lambda i, *_: (i, 0)))
        out_specs.append(pl.BlockSpec((rows // steps, cols), lambda i, *_: (i, 0)))
        out_shapes.append(jax.ShapeDtypeStruct(w.shape, BF16))
    return in_specs, out_specs, out_shapes


def _cast_slabs(in_refs, out_refs):
    for src, dst in zip(in_refs, out_refs):
        dst[...] = src[...].astype(BF16)


def _residue_rows(src, tmp, dil):
    n = src.shape[0]
    if dil <= MAX_SUBLANE_STRIDE:
        for r in range(dil):
            yield r, (src[pl.ds(r, n // dil, stride=dil), :] if dil > 1 else src[...])
        return
    outer = dil // MAX_SUBLANE_STRIDE
    assert outer <= MAX_SUBLANE_STRIDE
    for r0 in range(MAX_SUBLANE_STRIDE):
        tmp[r0] = src[pl.ds(r0, n // MAX_SUBLANE_STRIDE, stride=MAX_SUBLANE_STRIDE), :]
        for r1 in range(outer):
            yield r1 * MAX_SUBLANE_STRIDE + r0, tmp[r0, pl.ds(r1, n // dil, stride=outer), :]


def _qkv_kernel(x_ref, g_ref, wf_ref, *refs, n_cast):
    cast_in, refs = refs[:n_cast], refs[n_cast:]
    out_refs, cast_out = refs[:3 * N_GROUPS], refs[3 * N_GROUPS:3 * N_GROUPS + n_cast]
    w_ref, scr, scr2 = refs[-3:]

    @pl.when(pl.program_id(0) == 0)
    def _():
        w_ref[...] = wf_ref[...].astype(BF16)

    _cast_slabs(cast_in, cast_out)
    for part in range(TM // SUB_QKV):
        rows = slice(part * SUB_QKV, (part + 1) * SUB_QKV)
        a = _rms(x_ref[rows, :], g_ref[...]).astype(BF16)
        res = jnp.dot(a, w_ref[...], preferred_element_type=F32)
        for c in range(3 * ATTN_W // LANES):
            scr[part, c] = res[:, c * LANES:(c + 1) * LANES]
        for which in range(3):
            for gi, (_, dil) in enumerate(DIL_GROUPS):
                slab = (which * ATTN_W + gi * GROUP_W) // LANES
                out = out_refs[gi * 3 + which]
                out_rows = slice(part * SUB_QKV // dil, (part + 1) * SUB_QKV // dil)
                for half in range(HALVES):
                    src = scr.at[part, slab + half]
                    for r, val in _residue_rows(src, scr2.at[part, which * HALVES + half], dil):
                        if which == 0:
                            val = val * (HEAD_DIM ** -0.5 * LOG2E)
                        lo = r * GROUP_W + half * LANES
                        out[out_rows, lo:lo + LANES] = val.astype(BF16)


def _qkv_proj(x2, g_mix, w_in, cast_weights):
    T = x2.shape[0]
    steps = T // TM
    out_shapes, out_specs = [], []
    for _, dil in DIL_GROUPS:
        for _ in range(3):
            out_shapes.append(jax.ShapeDtypeStruct((T // dil, dil * GROUP_W), BF16))
            out_specs.append(pl.BlockSpec((TM // dil, dil * GROUP_W), lambda i: (i, 0)))
    cast_in, cast_out, cast_shapes = _cast_specs(cast_weights, steps)
    return pl.pallas_call(
        functools.partial(_qkv_kernel, n_cast=len(cast_weights)),
        grid=(steps,),
        in_specs=[pl.BlockSpec((TM, D_MODEL), lambda i: (i, 0)),
                  _const_spec((1, D_MODEL)),
                  _const_spec((D_MODEL, 3 * ATTN_W))] + cast_in,
        out_specs=out_specs + cast_out,
        out_shape=out_shapes + cast_shapes,
        scratch_shapes=[pltpu.VMEM((D_MODEL, 3 * ATTN_W), BF16),
                        pltpu.VMEM((TM // SUB_QKV, 3 * ATTN_W // LANES, SUB_QKV, LANES), F32),
                        pltpu.VMEM((TM // SUB_QKV, 3 * HALVES, MAX_SUBLANE_STRIDE,
                                    SUB_QKV // MAX_SUBLANE_STRIDE, LANES), F32)],
        compiler_params=pltpu.CompilerParams(
            dimension_semantics=("arbitrary",), vmem_limit_bytes=VMEM_LIMIT_BIG),
        name="qkv_proj",
    )(x2, g_mix, w_in, *cast_weights)


def _attn_block(q, k2, v2, bias, no_prev, head_masks, low_lanes):
    zero = jnp.zeros_like(q)
    qs = jnp.concatenate([jnp.where(m, q, zero) for m in head_masks], axis=0)
    s = lax.dot_general(qs, k2, (((1,), (1,)), ((), ())),
                        preferred_element_type=F32) + bias
    if no_prev is not None:
        s = jnp.where(no_prev, -jnp.inf, s)
    mx = jnp.max(s, axis=-1, keepdims=True)
    e = jnp.exp2(s - mx)
    den = jnp.sum(e, axis=-1, keepdims=True)
    pv = jnp.dot(e.astype(BF16), v2, preferred_element_type=F32)

    def head_rows(x, h):
        return x[h * BLK:(h + 1) * BLK]

    acc, m_rep, l_rep = [], [], []
    for half in range(HALVES):
        ha, hb = 2 * half, 2 * half + 1
        cols = slice(half * LANES, (half + 1) * LANES)
        acc.append(jnp.where(low_lanes, head_rows(pv, ha)[:, cols], head_rows(pv, hb)[:, cols]))
        m_rep.append(jnp.where(low_lanes, head_rows(mx, ha), head_rows(mx, hb)))
        l_rep.append(jnp.where(low_lanes, head_rows(den, ha), head_rows(den, hb)))
    return acc, m_rep, l_rep


def _attn_kernel(*refs):
    in_refs, (y_ref, nat, stage) = refs[:6 * N_GROUPS], refs[6 * N_GROUPS:]
    first_tile = pl.program_id(1) == 0
    lane = lax.broadcasted_iota(jnp.int32, (1, GROUP_W), 1)
    head_masks = [(lane >= HEAD_DIM * h) & (lane < HEAD_DIM * (h + 1))
                  for h in range(HEADS_PER_GROUP)]
    low_lanes = lax.broadcasted_iota(jnp.int32, (1, LANES), 1) < HEAD_DIM
    key_col = lax.broadcasted_iota(jnp.int32, (1, 2 * BLK), 1)
    no_prev_first = jnp.logical_and(first_tile, key_col < BLK)

    def group_blocks(gi):
        q_ref, k_ref, kp_ref, v_ref, vp_ref, bias_ref = in_refs[6 * gi:6 * gi + 6]
        dil = DIL_GROUPS[gi][1]
        for r in range(dil):
            lanes = slice(r * GROUP_W, (r + 1) * GROUP_W)
            for i in range(ATTN_TILE // dil // BLK):
                cur = slice(i * BLK, (i + 1) * BLK)
                if i == 0:
                    kp, vp = kp_ref[:, lanes], vp_ref[:, lanes]
                else:
                    prev = slice((i - 1) * BLK, i * BLK)
                    kp, vp = k_ref[prev, lanes], v_ref[prev, lanes]
                k2 = jnp.concatenate([kp, k_ref[cur, lanes]], axis=0)
                v2 = jnp.concatenate([vp, v_ref[cur, lanes]], axis=0)
                yield r, i, _attn_block(q_ref[cur, lanes], k2, v2, bias_ref[...],
                                        no_prev_first if i == 0 else None,
                                        head_masks, low_lanes)

    for gi in range(1, N_GROUPS):
        dil = DIL_GROUPS[gi][1]
        if dil <= MAX_SUBLANE_STRIDE:
            for r, i, stats in group_blocks(gi):
                dst = pl.ds(i * BLK * dil + r, BLK, stride=dil)
                for kind, parts in enumerate(stats):
                    for half in range(HALVES):
                        nat[gi - 1, kind, half, dst, :] = parts[half]
            continue
        outer = dil // MAX_SUBLANE_STRIDE
        for r, i, stats in group_blocks(gi):
            r1, r0 = divmod(r, MAX_SUBLANE_STRIDE)
            dst = pl.ds(i * BLK * outer + r1, BLK, stride=outer)
            for kind, parts in enumerate(stats):
                for half in range(HALVES):
                    stage[kind, half, r0, dst, :] = parts[half]
        for kind in range(3):
            for half in range(HALVES):
                for r0 in range(MAX_SUBLANE_STRIDE):
                    dst = pl.ds(r0, ATTN_TILE // MAX_SUBLANE_STRIDE, stride=MAX_SUBLANE_STRIDE)
                    nat[gi - 1, kind, half, dst, :] = stage[kind, half, r0]

    for r, i, (acc0, m0, l0) in group_blocks(0):
        rows = slice(i * BLK, (i + 1) * BLK)
        for half in range(HALVES):
            accs = [acc0[half]] + [nat[g, 0, half, rows, :] for g in range(N_GROUPS - 1)]
            ms = [m0[half]] + [nat[g, 1, half, rows, :] for g in range(N_GROUPS - 1)]
            ls = [l0[half]] + [nat[g, 2, half, rows, :] for g in range(N_GROUPS - 1)]
            m_all = jnp.maximum(jnp.maximum(ms[0], ms[1]), ms[2])
            ws = [jnp.exp2(m - m_all) for m in ms]
            num = ws[0] * accs[0] + ws[1] * accs[1] + ws[2] * accs[2]
            den = ws[0] * ls[0] + ws[1] * ls[1] + ws[2] * ls[2]
            y_ref[rows, half * LANES:(half + 1) * LANES] = (num * (1.0 / den)).astype(BF16)


def _dilated_attention(qkv, batch):
    T = qkv[0].shape[0]
    tiles_per_batch = T // ATTN_TILE // batch

    def cur_map(b, t):
        return (b * tiles_per_batch + t, 0)

    operands, in_specs = [], []
    for gi, (_, dil) in enumerate(DIL_GROUPS):
        qd, kd, vd = qkv[3 * gi:3 * gi + 3]
        rows, width = ATTN_TILE // dil, dil * GROUP_W
        prev_blocks = rows // BLK

        def prev_map(b, t, prev_blocks=prev_blocks):
            return (jnp.maximum((b * tiles_per_batch + t) * prev_blocks - 1, 0), 0)

        bias = jnp.asarray(_attn_bias(gi))
        operands += [qd, kd, kd, vd, vd, bias]
        in_specs += [pl.BlockSpec((rows, width), cur_map),
                     pl.BlockSpec((rows, width), cur_map),
                     pl.BlockSpec((BLK, width), prev_map),
                     pl.BlockSpec((rows, width), cur_map),
                     pl.BlockSpec((BLK, width), prev_map),
                     _const_spec(bias.shape)]
    return pl.pallas_call(
        _attn_kernel,
        grid=(batch, tiles_per_batch),
        in_specs=in_specs,
        out_specs=pl.BlockSpec((ATTN_TILE, GROUP_W), cur_map),
        out_shape=jax.ShapeDtypeStruct((T, GROUP_W), BF16),
        scratch_shapes=[pltpu.VMEM((N_GROUPS - 1, 3, HALVES, ATTN_TILE, LANES), F32),
                        pltpu.VMEM((3, HALVES, MAX_SUBLANE_STRIDE,
                                    ATTN_TILE // MAX_SUBLANE_STRIDE, LANES), F32)],
        compiler_params=pltpu.CompilerParams(
            dimension_semantics=("arbitrary", "arbitrary"), vmem_limit_bytes=VMEM_LIMIT_BIG),
        name="dilated_attn",
    )(*operands)


def _mixer_kernel(x_ref, y_ref, g_ref, win_ref, bg_ref, ws_ref, bs_ref, gs_ref,
                  wba_ref, wbs_ref, wo_ref, *refs, n_cast):
    cast_in, h_ref, cast_out = refs[:n_cast], refs[n_cast], refs[n_cast + 1:]
    _cast_slabs(cast_in, cast_out)
    row = lax.broadcasted_iota(jnp.int32, (SGU_CHUNK, SGU_CHUNK), 0)
    col = lax.broadcasted_iota(jnp.int32, (SGU_CHUNK, SGU_CHUNK), 1)
    ws = [jnp.where(row >= col, ws_ref[g], 0.0).astype(BF16) for g in range(SGU_GROUPS)]
    for part in range(TM // SUB_MIX):
        rows = slice(part * SUB_MIX, (part + 1) * SUB_MIX)
        x = x_ref[rows, :]
        a = _rms(x, g_ref[...]).astype(BF16)
        proj = jnp.dot(a, win_ref[:, 3 * ATTN_W:],
                       preferred_element_type=F32)

        z = jax.nn.gelu(proj[:, :2 * SGU_W], approximate=True)
        u = z[:, :SGU_W]
        v = _rms(z[:, SGU_W:], gs_ref[...]).astype(BF16)
        chunks = []
        for c in range(SUB_MIX // SGU_CHUNK):
            rs = slice(c * SGU_CHUNK, (c + 1) * SGU_CHUNK)
            parts = [jnp.dot(ws[g], v[rs, g * SGU_GROUP_W:(g + 1) * SGU_GROUP_W],
                             preferred_element_type=F32) for g in range(SGU_GROUPS)]
            chunks.append(jnp.concatenate(parts, axis=1) + bs_ref[...])
        y_sgu = u * jnp.concatenate(chunks, axis=0)

        gates = 1.0 / (1.0 + jnp.exp(-(proj[:, 2 * SGU_W:] + bg_ref[...])))
        merged = (gates[:, :D_MODEL] * jnp.dot(y_ref[rows, :], wba_ref[...],
                                                preferred_element_type=F32)
                  + gates[:, D_MODEL:] * jnp.dot(y_sgu.astype(BF16), wbs_ref[...],
                                                 preferred_element_type=F32))
        h_ref[rows, :] = x + jnp.dot(merged.astype(BF16), wo_ref[...],
                                     preferred_element_type=F32)


def _mixer(x2, y_attn, g_mix, w_in_b, b_gate, w_s, bs_b, g_sgu, w_ba, w_bs, w_out, cast_weights):
    T = x2.shape[0]
    steps = T // TM
    row_spec = lambda w: pl.BlockSpec((TM, w), lambda i: (i, 0))
    consts = (g_mix, w_in_b, b_gate, w_s, bs_b, g_sgu, w_ba, w_bs, w_out)
    cast_in, cast_out, cast_shapes = _cast_specs(cast_weights, steps)
    return pl.pallas_call(
        functools.partial(_mixer_kernel, n_cast=len(cast_weights)),
        grid=(steps,),
        in_specs=([row_spec(D_MODEL), row_spec(GROUP_W)]
                  + [_const_spec(c.shape) for c in consts] + cast_in),
        out_specs=[row_spec(D_MODEL)] + cast_out,
        out_shape=[jax.ShapeDtypeStruct((T, D_MODEL), F32)] + cast_shapes,
        compiler_params=pltpu.CompilerParams(
            dimension_semantics=("arbitrary",), vmem_limit_bytes=VMEM_LIMIT_BIG),
        name="mixer_tail",
    )(x2, y_attn, *consts, *cast_weights)


def _mem_kv_kernel(m_ref, g_ref, w_ref, kv_ref):
    m = _rms(m_ref[...], g_ref[...]).astype(BF16)
    kv_ref[...] = jnp.dot(m, w_ref[...], preferred_element_type=F32).astype(BF16)


def _mem_kv(mem2, g_mem, w_kv):
    rows = mem2.shape[0]
    return pl.pallas_call(
        _mem_kv_kernel,
        grid=(1,),
        in_specs=[_const_spec(mem2.shape), _const_spec(g_mem.shape), _const_spec(w_kv.shape)],
        out_specs=pl.BlockSpec((rows, 2 * MEM_W), lambda i: (0, 0)),
        out_shape=jax.ShapeDtypeStruct((rows, 2 * MEM_W), BF16),
        compiler_params=pltpu.CompilerParams(
            dimension_semantics=("arbitrary",), vmem_limit_bytes=VMEM_LIMIT),
        name="mem_kv",
    )(mem2, g_mem, w_kv)


def _cross_kernel(h_ref, kv_ref, g_ref, wq_ref, wo_ref, out_ref):
    for part in range(TM // SUB_CROSS):
        rows = slice(part * SUB_CROSS, (part + 1) * SUB_CROSS)
        h = h_ref[rows, :]
        c = _rms(h, g_ref[...]).astype(BF16)
        q = jnp.dot(c, wq_ref[...], preferred_element_type=F32).astype(BF16)
        heads = []
        for hd in range(MEM_HEADS):
            cols = slice(hd * MEM_HEAD_DIM, (hd + 1) * MEM_HEAD_DIM)
            k = kv_ref[:, cols]
            v = kv_ref[:, MEM_W + hd * MEM_HEAD_DIM:MEM_W + (hd + 1) * MEM_HEAD_DIM]
            s = lax.dot_general(q[:, cols], k, (((1,), (1,)), ((), ())),
                                preferred_element_type=F32) * (MEM_HEAD_DIM ** -0.5)
            e = jnp.exp(s - jnp.max(s, axis=-1, keepdims=True))
            den = jnp.sum(e, axis=-1, keepdims=True)
            o = jnp.dot(e.astype(BF16), v, preferred_element_type=F32) * (1.0 / den)
            heads.append(o.astype(BF16))
        o = jnp.concatenate(heads, axis=1)
        out_ref[rows, :] = h + jnp.dot(o, wo_ref[...], preferred_element_type=F32)


def _cross_attention(h, kv, g_cross, w_q, w_o, batch):
    T = h.shape[0]
    blocks_per_batch = T // batch // TM
    mem_len = kv.shape[0] // batch
    return pl.pallas_call(
        _cross_kernel,
        grid=(batch, blocks_per_batch),
        in_specs=[pl.BlockSpec((TM, D_MODEL), lambda b, i: (b * blocks_per_batch + i, 0)),
                  pl.BlockSpec((mem_len, 2 * MEM_W), lambda b, i: (b, 0)),
                  _const_spec(g_cross.shape), _const_spec(w_q.shape), _const_spec(w_o.shape)],
        out_specs=pl.BlockSpec((TM, D_MODEL), lambda b, i: (b * blocks_per_batch + i, 0)),
        out_shape=jax.ShapeDtypeStruct((T, D_MODEL), F32),
        compiler_params=pltpu.CompilerParams(
            dimension_semantics=("arbitrary", "arbitrary"), vmem_limit_bytes=VMEM_LIMIT),
        name="cross_attn",
    )(h, kv, g_cross, w_q, w_o)


def _ffn_kernel(h_ref, g_ref, wgu_ref, wd_ref, gf_ref, out_ref):
    for part in range(TM // SUB_FFN):
        rows = slice(part * SUB_FFN, (part + 1) * SUB_FFN)
        h = h_ref[rows, :]
        f = _rms(h, g_ref[...]).astype(BF16)
        gu = jnp.dot(f, wgu_ref[...], preferred_element_type=F32)
        gt, up = gu[:, :D_FF], gu[:, D_FF:]
        act = (gt * (1.0 / (1.0 + jnp.exp(-gt))) * up).astype(BF16)
        acc = h + jnp.dot(act, wd_ref[...], preferred_element_type=F32)
        out_ref[rows, :] = _rms(acc, gf_ref[...])


def _ffn(h, g_ffn, w_gu, w_down, g_final):
    T = h.shape[0]
    consts = (g_ffn, w_gu, w_down, g_final)
    return pl.pallas_call(
        _ffn_kernel,
        grid=(T // TM,),
        in_specs=[pl.BlockSpec((TM, D_MODEL), lambda i: (i, 0))]
                 + [_const_spec(c.shape) for c in consts],
        out_specs=pl.BlockSpec((TM, D_MODEL), lambda i: (i, 0)),
        out_shape=jax.ShapeDtypeStruct((T, D_MODEL), F32),
        compiler_params=pltpu.CompilerParams(
            dimension_semantics=("arbitrary",), vmem_limit_bytes=VMEM_LIMIT_BIG),
        name="ffn",
    )(h, *consts)


def kernel(x, mem, g_mix, w_in, b_gate, w_sgu_spatial, b_sgu_spatial, g_sgu, w_branch_attn,
           w_branch_sgu, w_out, g_cross, g_mem, w_q_cross, w_kv_cross, w_o_cross, g_ffn,
           w_gate_up, w_down, g_final):
    B, S, D = x.shape
    assert D == D_MODEL and S % ATTN_TILE == 0 and w_in.shape[0] == 1
    T = B * S
    x2 = x.reshape(T, D)
    row = lambda v: v.reshape(1, -1)

    g_mix2 = row(g_mix[0])
    w_in2 = w_in.reshape(D, w_in.shape[-1])

    *qkv, w_in_b, w_ba, w_bs, w_o1 = _qkv_proj(
        x2, g_mix2, w_in2, (w_in2, w_branch_attn[0], w_branch_sgu[0], w_out[0]))
    y_attn = _dilated_attention(qkv, batch=B)

    bs_b = jnp.repeat(b_sgu_spatial[0].T, SGU_GROUP_W, axis=1)
    h, w_q, w_kv, w_o2, w_gu, w_dn = _mixer(
        x2, y_attn, g_mix2, w_in_b, row(b_gate[0]), w_sgu_spatial[0], bs_b, row(g_sgu[0]),
        w_ba, w_bs, w_o1,
        (w_q_cross[0], w_kv_cross[0], w_o_cross[0], w_gate_up[0], w_down[0]))

    kv = _mem_kv(mem.reshape(B * mem.shape[1], D), row(g_mem[0]), w_kv)
    h = _cross_attention(h, kv, row(g_cross[0]), w_q, w_o2, batch=B)
    out = _ffn(h, row(g_ffn[0]), w_gu, w_dn, row(g_final))
    return out.reshape(B, S, D)
```

```python
import functools
import math

import jax
import jax.numpy as jnp
import numpy as np
from jax import lax
from jax.experimental import pallas as pl
from jax.experimental.pallas import tpu as pltpu

D_MODEL = 1024
HEAD_DIM = 64
DIL_GROUPS = ((128, 1), (512, 4), (2048, 16))
N_GROUPS = 3
HEADS_PER_GROUP = 4
N_ATTN_HEADS = N_GROUPS * HEADS_PER_GROUP
ATTN_W = N_ATTN_HEADS * HEAD_DIM
GROUP_W = HEADS_PER_GROUP * HEAD_DIM
BLK = 128
LANES = 128
BF16_SUBLANES = 16
MAX_SUBLANE_STRIDE = 4
HALVES = GROUP_W // LANES
SGU_CHUNK = 128
SGU_GROUPS = 4
SGU_W = 512
SGU_GROUP_W = SGU_W // SGU_GROUPS
N_BRANCH = 2
MEM_HEADS = 4
MEM_HEAD_DIM = 128
MEM_W = MEM_HEADS * MEM_HEAD_DIM
D_FF = 2816
EPS = 1e-6
LOG2E = math.log2(math.e)

ATTN_TILE = 2048
TM = 1024
TM_CROSS = 2048
SUB_QKV = 512
SUB_MIX = 512
SUB_CROSS = 512
SUB_FFN = 256
VMEM_LIMIT = 48 * 1024 * 1024
VMEM_LIMIT_BIG = 56 * 1024 * 1024

F32 = jnp.float32
BF16 = jnp.bfloat16


def _rms(xf, g):
    r = lax.rsqrt(jnp.mean(xf * xf, axis=-1, keepdims=True) + EPS)
    return xf * r * g


def _alibi_slopes_grouped():
    def pow2(n):
        start = 2.0 ** (-8.0 / n)
        return [start ** (i + 1) for i in range(n)]
    n = N_ATTN_HEADS
    if math.log2(n).is_integer():
        s = pow2(n)
    else:
        c = 2 ** int(math.floor(math.log2(n)))
        s = pow2(c) + pow2(2 * c)[0::2][: n - c]
    s = np.array(sorted(s, reverse=True), dtype=np.float32)
    return s.reshape(N_GROUPS, HEADS_PER_GROUP)


def _attn_bias(gi):
    window, dil = DIL_GROUPS[gi]
    n_back = window // dil
    steps = (np.arange(BLK)[:, None] + BLK) - np.arange(2 * BLK)[None, :]
    band = (steps >= 0) & (steps <= n_back)
    dist = (np.clip(steps, 0, None) * dil).astype(np.float32)
    slopes = _alibi_slopes_grouped()[gi]
    bias = -slopes[:, None, None] * dist[None]
    bias = np.where(band[None], bias.astype(np.float64) * LOG2E, -np.inf).astype(np.float32)
    return bias.reshape(HEADS_PER_GROUP * BLK, 2 * BLK)


def _const_spec(shape):
    nd = len(shape)
    return pl.BlockSpec(shape, lambda *_: (0,) * nd, pipeline_mode=pl.Buffered(1))


def _cast_specs(weights, steps):
    in_specs, out_specs, out_shapes = [], [], []
    for w in weights:
        rows, cols = w.shape
        assert rows % (steps * BF16_SUBLANES) == 0
        in_specs.append(pl.BlockSpec((rows // steps, cols), lambda i, *_: (i, 0)))
        out_specs.append(pl.BlockSpec((rows // steps, cols), lambda i, *_: (i, 0)))
        out_shapes.append(jax.ShapeDtypeStruct(w.shape, BF16))
    return in_specs, out_specs, out_shapes


def _cast_slabs(in_refs, out_refs):
    for src, dst in zip(in_refs, out_refs):
        dst[...] = src[...].astype(BF16)


def _residue_rows(src, tmp, dil):
    n = src.shape[0]
    if dil <= MAX_SUBLANE_STRIDE:
        for r in range(dil):
            yield r, (src[pl.ds(r, n // dil, stride=dil), :] if dil > 1 else src[...])
        return
    outer = dil // MAX_SUBLANE_STRIDE
    assert outer <= MAX_SUBLANE_STRIDE
    for r0 in range(MAX_SUBLANE_STRIDE):
        tmp[r0] = src[pl.ds(r0, n // MAX_SUBLANE_STRIDE, stride=MAX_SUBLANE_STRIDE), :]
        for r1 in range(outer):
            yield r1 * MAX_SUBLANE_STRIDE + r0, tmp[r0, pl.ds(r1, n // dil, stride=outer), :]


def _qkv_kernel(x_ref, g_ref, wf_ref, *refs, n_cast):
    cast_in, refs = refs[:n_cast], refs[n_cast:]
    out_refs, cast_out = refs[:3 * N_GROUPS], refs[3 * N_GROUPS:3 * N_GROUPS + n_cast]
    w_ref, scr, scr2 = refs[-3:]

    @pl.when(pl.program_id(0) == 0)
    def _():
        w_ref[...] = wf_ref[...].astype(BF16)

    _cast_slabs(cast_in, cast_out)
    for part in range(TM // SUB_QKV):
        rows = slice(part * SUB_QKV, (part + 1) * SUB_QKV)
        a = _rms(x_ref[rows, :], g_ref[...]).astype(BF16)
        res = jnp.dot(a, w_ref[...], preferred_element_type=F32)
        for which in range(3):
            for gi, (_, dil) in enumerate(DIL_GROUPS):
                slab = (which * ATTN_W + gi * GROUP_W) // LANES
                out = out_refs[gi * 3 + which]
                out_rows = slice(part * SUB_QKV // dil, (part + 1) * SUB_QKV // dil)
                for half in range(HALVES):
                    cols = slice((slab + half) * LANES, (slab + half + 1) * LANES)
                    if dil == 1:
                        pieces = [(0, res[:, cols])]
                    else:
                        src = scr.at[part, slab + half]
                        src[...] = res[:, cols]
                        pieces = _residue_rows(src, scr2.at[part, which * HALVES + half], dil)
                    for r, val in pieces:
                        if which == 0:
                            val = val * (HEAD_DIM ** -0.5 * LOG2E)
                        lo = r * GROUP_W + half * LANES
                        out[out_rows, lo:lo + LANES] = val.astype(BF16)


def _qkv_proj(x2, g_mix, w_in, cast_weights):
    T = x2.shape[0]
    steps = T // TM
    out_shapes, out_specs = [], []
    for _, dil in DIL_GROUPS:
        for _ in range(3):
            out_shapes.append(jax.ShapeDtypeStruct((T // dil, dil * GROUP_W), BF16))
            out_specs.append(pl.BlockSpec((TM // dil, dil * GROUP_W), lambda i: (i, 0)))
    cast_in, cast_out, cast_shapes = _cast_specs(cast_weights, steps)
    return pl.pallas_call(
        functools.partial(_qkv_kernel, n_cast=len(cast_weights)),
        grid=(steps,),
        in_specs=[pl.BlockSpec((TM, D_MODEL), lambda i: (i, 0)),
                  _const_spec((1, D_MODEL)),
                  _const_spec((D_MODEL, 3 * ATTN_W))] + cast_in,
        out_specs=out_specs + cast_out,
        out_shape=out_shapes + cast_shapes,
        scratch_shapes=[pltpu.VMEM((D_MODEL, 3 * ATTN_W), BF16),
                        pltpu.VMEM((TM // SUB_QKV, 3 * ATTN_W // LANES, SUB_QKV, LANES), F32),
                        pltpu.VMEM((TM // SUB_QKV, 3 * HALVES, MAX_SUBLANE_STRIDE,
                                    SUB_QKV // MAX_SUBLANE_STRIDE, LANES), F32)],
        compiler_params=pltpu.CompilerParams(
            dimension_semantics=("arbitrary",), vmem_limit_bytes=VMEM_LIMIT_BIG),
        name="qkv_proj",
    )(x2, g_mix, w_in, *cast_weights)


def _attn_block(q, k2, v2, bias, no_prev, head_masks, low_lanes):
    zero = jnp.zeros_like(q)
    qs = jnp.concatenate([jnp.where(m, q, zero) for m in head_masks], axis=0)
    s = lax.dot_general(qs, k2, (((1,), (1,)), ((), ())),
                        preferred_element_type=F32) + bias
    if no_prev is not None:
        s = jnp.where(no_prev, -jnp.inf, s)
    mx = jnp.max(s, axis=-1, keepdims=True)
    e = jnp.exp2(s - mx)
    den = jnp.sum(e, axis=-1, keepdims=True)
    pv = jnp.dot(e.astype(BF16), v2, preferred_element_type=F32)

    def head_rows(x, h):
        return x[h * BLK:(h + 1) * BLK]

    acc, m_rep, l_rep = [], [], []
    for half in range(HALVES):
        ha, hb = 2 * half, 2 * half + 1
        cols = slice(half * LANES, (half + 1) * LANES)
        acc.append(jnp.where(low_lanes, head_rows(pv, ha)[:, cols], head_rows(pv, hb)[:, cols]))
        m_rep.append(jnp.where(low_lanes, head_rows(mx, ha), head_rows(mx, hb)))
        l_rep.append(jnp.where(low_lanes, head_rows(den, ha), head_rows(den, hb)))
    return acc, m_rep, l_rep


def _attn_kernel(*refs):
    in_refs, (y_ref, nat, stage) = refs[:6 * N_GROUPS], refs[6 * N_GROUPS:]
    first_tile = pl.program_id(1) == 0
    lane = lax.broadcasted_iota(jnp.int32, (1, GROUP_W), 1)
    head_masks = [(lane >= HEAD_DIM * h) & (lane < HEAD_DIM * (h + 1))
                  for h in range(HEADS_PER_GROUP)]
    low_lanes = lax.broadcasted_iota(jnp.int32, (1, LANES), 1) < HEAD_DIM
    key_col = lax.broadcasted_iota(jnp.int32, (1, 2 * BLK), 1)
    no_prev_first = jnp.logical_and(first_tile, key_col < BLK)

    def group_blocks(gi):
        q_ref, k_ref, kp_ref, v_ref, vp_ref, bias_ref = in_refs[6 * gi:6 * gi + 6]
        dil = DIL_GROUPS[gi][1]
        for r in range(dil):
            lanes = slice(r * GROUP_W, (r + 1) * GROUP_W)
            for i in range(ATTN_TILE // dil // BLK):
                cur = slice(i * BLK, (i + 1) * BLK)
                if i == 0:
                    kp, vp = kp_ref[:, lanes], vp_ref[:, lanes]
                else:
                    prev = slice((i - 1) * BLK, i * BLK)
                    kp, vp = k_ref[prev, lanes], v_ref[prev, lanes]
                k2 = jnp.concatenate([kp, k_ref[cur, lanes]], axis=0)
                v2 = jnp.concatenate([vp, v_ref[cur, lanes]], axis=0)
                yield r, i, _attn_block(q_ref[cur, lanes], k2, v2, bias_ref[...],
                                        no_prev_first if i == 0 else None,
                                        head_masks, low_lanes)

    for gi in range(1, N_GROUPS):
        dil = DIL_GROUPS[gi][1]
        if dil <= MAX_SUBLANE_STRIDE:
            for r, i, stats in group_blocks(gi):
                dst = pl.ds(i * BLK * dil + r, BLK, stride=dil)
                for kind, parts in enumerate(stats):
                    for half in range(HALVES):
                        nat[gi - 1, kind, half, dst, :] = parts[half]
            continue
        outer = dil // MAX_SUBLANE_STRIDE
        for r, i, stats in group_blocks(gi):
            r1, r0 = divmod(r, MAX_SUBLANE_STRIDE)
            dst = pl.ds(i * BLK * outer + r1, BLK, stride=outer)
            for kind, parts in enumerate(stats):
                for half in range(HALVES):
                    stage[kind, half, r0, dst, :] = parts[half]
        for kind in range(3):
            for half in range(HALVES):
                for r0 in range(MAX_SUBLANE_STRIDE):
                    dst = pl.ds(r0, ATTN_TILE // MAX_SUBLANE_STRIDE, stride=MAX_SUBLANE_STRIDE)
                    nat[gi - 1, kind, half, dst, :] = stage[kind, half, r0]

    for r, i, (acc0, m0, l0) in group_blocks(0):
        rows = slice(i * BLK, (i + 1) * BLK)
        for half in range(HALVES):
            accs = [acc0[half]] + [nat[g, 0, half, rows, :] for g in range(N_GROUPS - 1)]
            ms = [m0[half]] + [nat[g, 1, half, rows, :] for g in range(N_GROUPS - 1)]
            ls = [l0[half]] + [nat[g, 2, half, rows, :] for g in range(N_GROUPS - 1)]
            m_all = jnp.maximum(jnp.maximum(ms[0], ms[1]), ms[2])
            ws = [jnp.exp2(m - m_all) for m in ms]
            num = ws[0] * accs[0] + ws[1] * accs[1] + ws[2] * accs[2]
            den = ws[0] * ls[0] + ws[1] * ls[1] + ws[2] * ls[2]
            y_ref[rows, half * LANES:(half + 1) * LANES] = (num * (1.0 / den)).astype(BF16)


def _dilated_attention(qkv, batch):
    T = qkv[0].shape[0]
    tiles_per_batch = T // ATTN_TILE // batch

    def cur_map(b, t):
        return (b * tiles_per_batch + t, 0)

    operands, in_specs = [], []
    for gi, (_, dil) in enumerate(DIL_GROUPS):
        qd, kd, vd = qkv[3 * gi:3 * gi + 3]
        rows, width = ATTN_TILE // dil, dil * GROUP_W
        prev_blocks = rows // BLK

        def prev_map(b, t, prev_blocks=prev_blocks):
            return (jnp.maximum((b * tiles_per_batch + t) * prev_blocks - 1, 0), 0)

        bias = jnp.asarray(_attn_bias(gi))
        operands += [qd, kd, kd, vd, vd, bias]
        in_specs += [pl.BlockSpec((rows, width), cur_map),
                     pl.BlockSpec((rows, width), cur_map),
                     pl.BlockSpec((BLK, width), prev_map),
                     pl.BlockSpec((rows, width), cur_map),
                     pl.BlockSpec((BLK, width), prev_map),
                     _const_spec(bias.shape)]
    return pl.pallas_call(
        _attn_kernel,
        grid=(batch, tiles_per_batch),
        in_specs=in_specs,
        out_specs=pl.BlockSpec((ATTN_TILE, GROUP_W), cur_map),
        out_shape=jax.ShapeDtypeStruct((T, GROUP_W), BF16),
        scratch_shapes=[pltpu.VMEM((N_GROUPS - 1, 3, HALVES, ATTN_TILE, LANES), F32),
                        pltpu.VMEM((3, HALVES, MAX_SUBLANE_STRIDE,
                                    ATTN_TILE // MAX_SUBLANE_STRIDE, LANES), F32)],
        compiler_params=pltpu.CompilerParams(
            dimension_semantics=("arbitrary", "arbitrary"), vmem_limit_bytes=VMEM_LIMIT_BIG),
        name="dilated_attn",
    )(*operands)


def _mixer_kernel(x_ref, y_ref, g_ref, win_ref, bg_ref, ws_ref, bs_ref, gs_ref,
                  wba_ref, wbs_ref, wo_ref, *refs, n_cast):
    cast_in, h_ref, cast_out = refs[:n_cast], refs[n_cast], refs[n_cast + 1:]
    _cast_slabs(cast_in, cast_out)
    row = lax.broadcasted_iota(jnp.int32, (SGU_CHUNK, SGU_CHUNK), 0)
    col = lax.broadcasted_iota(jnp.int32, (SGU_CHUNK, SGU_CHUNK), 1)
    ws = [jnp.where(row >= col, ws_ref[g], 0.0).astype(BF16) for g in range(SGU_GROUPS)]
    for part in range(TM // SUB_MIX):
        rows = slice(part * SUB_MIX, (part + 1) * SUB_MIX)
        x = x_ref[rows, :]
        a = _rms(x, g_ref[...]).astype(BF16)
        proj = jnp.dot(a, win_ref[:, 3 * ATTN_W:],
                       preferred_element_type=F32)

        z = jax.nn.gelu(proj[:, :2 * SGU_W], approximate=True)
        u = z[:, :SGU_W]
        v = _rms(z[:, SGU_W:], gs_ref[...]).astype(BF16)
        n_chunks = SUB_MIX // SGU_CHUNK
        mixed = []
        for g in range(SGU_GROUPS):
            gl = slice(g * SGU_GROUP_W, (g + 1) * SGU_GROUP_W)
            vg = jnp.concatenate([v[c * SGU_CHUNK:(c + 1) * SGU_CHUNK, gl]
                                  for c in range(n_chunks)], axis=1)
            mixed.append(jnp.dot(ws[g], vg, preferred_element_type=F32))
        chunks = [jnp.concatenate([m[:, c * SGU_GROUP_W:(c + 1) * SGU_GROUP_W] for m in mixed],
                                  axis=1) + bs_ref[...] for c in range(n_chunks)]
        y_sgu = u * jnp.concatenate(chunks, axis=0)

        gates = 1.0 / (1.0 + jnp.exp(-(proj[:, 2 * SGU_W:] + bg_ref[...])))
        merged = (gates[:, :D_MODEL] * jnp.dot(y_ref[rows, :], wba_ref[...],
                                                preferred_element_type=F32)
                  + gates[:, D_MODEL:] * jnp.dot(y_sgu.astype(BF16), wbs_ref[...],
                                                 preferred_element_type=F32))
        h_ref[rows, :] = x + jnp.dot(merged.astype(BF16), wo_ref[...],
                                     preferred_element_type=F32)


def _mixer(x2, y_attn, g_mix, w_in_b, b_gate, w_s, bs_b, g_sgu, w_ba, w_bs, w_out, cast_weights):
    T = x2.shape[0]
    steps = T // TM
    row_spec = lambda w: pl.BlockSpec((TM, w), lambda i: (i, 0))
    consts = (g_mix, w_in_b, b_gate, w_s, bs_b, g_sgu, w_ba, w_bs, w_out)
    cast_in, cast_out, cast_shapes = _cast_specs(cast_weights, steps)
    return pl.pallas_call(
        functools.partial(_mixer_kernel, n_cast=len(cast_weights)),
        grid=(steps,),
        in_specs=([row_spec(D_MODEL), row_spec(GROUP_W)]
                  + [_const_spec(c.shape) for c in consts] + cast_in),
        out_specs=[row_spec(D_MODEL)] + cast_out,
        out_shape=[jax.ShapeDtypeStruct((T, D_MODEL), F32)] + cast_shapes,
        compiler_params=pltpu.CompilerParams(
            dimension_semantics=("arbitrary",), vmem_limit_bytes=VMEM_LIMIT_BIG),
        name="mixer_tail",
    )(x2, y_attn, *consts, *cast_weights)


def _cross_kernel(h_ref, mem_ref, gm_ref, wkv_ref, g_ref, wq_ref, wo_ref, out_ref, kv_ref):
    @pl.when(pl.program_id(1) == 0)
    def _():
        m = _rms(mem_ref[...], gm_ref[...]).astype(BF16)
        kv_ref[...] = jnp.dot(m, wkv_ref[...], preferred_element_type=F32).astype(BF16)

    for part in range(TM_CROSS // SUB_CROSS):
        rows = slice(part * SUB_CROSS, (part + 1) * SUB_CROSS)
        h = h_ref[rows, :]
        c = _rms(h, g_ref[...]).astype(BF16)
        q = jnp.dot(c, wq_ref[...], preferred_element_type=F32).astype(BF16)
        heads = []
        for hd in range(MEM_HEADS):
            cols = slice(hd * MEM_HEAD_DIM, (hd + 1) * MEM_HEAD_DIM)
            k = kv_ref[:, cols]
            v = kv_ref[:, MEM_W + hd * MEM_HEAD_DIM:MEM_W + (hd + 1) * MEM_HEAD_DIM]
            s = lax.dot_general(q[:, cols], k, (((1,), (1,)), ((), ())),
                                preferred_element_type=F32) * (MEM_HEAD_DIM ** -0.5)
            e = jnp.exp(s - jnp.max(s, axis=-1, keepdims=True))
            den = jnp.sum(e, axis=-1, keepdims=True)
            o = jnp.dot(e.astype(BF16), v, preferred_element_type=F32) * (1.0 / den)
            heads.append(o.astype(BF16))
        o = jnp.concatenate(heads, axis=1)
        out_ref[rows, :] = h + jnp.dot(o, wo_ref[...], preferred_element_type=F32)


def _cross_attention(h, mem2, g_mem, w_kv, g_cross, w_q, w_o, batch):
    T = h.shape[0]
    blocks_per_batch = T // batch // TM_CROSS
    mem_len = mem2.shape[0] // batch
    consts = (g_mem, w_kv, g_cross, w_q, w_o)
    row_map = lambda b, i: (b * blocks_per_batch + i, 0)
    return pl.pallas_call(
        _cross_kernel,
        grid=(batch, blocks_per_batch),
        in_specs=[pl.BlockSpec((TM_CROSS, D_MODEL), row_map),
                  pl.BlockSpec((mem_len, D_MODEL), lambda b, i: (b, 0))]
                 + [_const_spec(c.shape) for c in consts],
        out_specs=pl.BlockSpec((TM_CROSS, D_MODEL), row_map),
        out_shape=jax.ShapeDtypeStruct((T, D_MODEL), F32),
        scratch_shapes=[pltpu.VMEM((mem_len, 2 * MEM_W), BF16)],
        compiler_params=pltpu.CompilerParams(
            dimension_semantics=("arbitrary", "arbitrary"), vmem_limit_bytes=VMEM_LIMIT),
        name="cross_attn",
    )(h, mem2, *consts)


def _ffn_kernel(h_ref, g_ref, wgu_ref, wd_ref, gf_ref, out_ref):
    for part in range(TM // SUB_FFN):
        rows = slice(part * SUB_FFN, (part + 1) * SUB_FFN)
        h = h_ref[rows, :]
        f = _rms(h, g_ref[...]).astype(BF16)
        gu = jnp.dot(f, wgu_ref[...], preferred_element_type=F32)
        gt, up = gu[:, :D_FF], gu[:, D_FF:]
        act = (gt * (1.0 / (1.0 + jnp.exp(-gt))) * up).astype(BF16)
        acc = h + jnp.dot(act, wd_ref[...], preferred_element_type=F32)
        out_ref[rows, :] = _rms(acc, gf_ref[...])


def _ffn(h, g_ffn, w_gu, w_down, g_final):
    T = h.shape[0]
    consts = (g_ffn, w_gu, w_down, g_final)
    return pl.pallas_call(
        _ffn_kernel,
        grid=(T // TM,),
        in_specs=[pl.BlockSpec((TM, D_MODEL), lambda i: (i, 0))]
                 + [_const_spec(c.shape) for c in consts],
        out_specs=pl.BlockSpec((TM, D_MODEL), lambda i: (i, 0)),
        out_shape=jax.ShapeDtypeStruct((T, D_MODEL), F32),
        compiler_params=pltpu.CompilerParams(
            dimension_semantics=("arbitrary",), vmem_limit_bytes=VMEM_LIMIT_BIG),
        name="ffn",
    )(h, *consts)


def kernel(x, mem, g_mix, w_in, b_gate, w_sgu_spatial, b_sgu_spatial, g_sgu, w_branch_attn,
           w_branch_sgu, w_out, g_cross, g_mem, w_q_cross, w_kv_cross, w_o_cross, g_ffn,
           w_gate_up, w_down, g_final):
    B, S, D = x.shape
    assert D == D_MODEL and S % ATTN_TILE == 0 and w_in.shape[0] == 1
    T = B * S
    x2 = x.reshape(T, D)
    row = lambda v: v.reshape(1, -1)

    g_mix2 = row(g_mix[0])
    w_in2 = w_in.reshape(D, w_in.shape[-1])

    *qkv, w_in_b, w_ba, w_bs, w_o1 = _qkv_proj(
        x2, g_mix2, w_in2, (w_in2, w_branch_attn[0], w_branch_sgu[0], w_out[0]))
    y_attn = _dilated_attention(qkv, batch=B)

    bs_b = jnp.repeat(b_sgu_spatial[0].T, SGU_GROUP_W, axis=1)
    h, w_q, w_kv, w_o2, w_gu, w_dn = _mixer(
        x2, y_attn, g_mix2, w_in_b, row(b_gate[0]), w_sgu_spatial[0], bs_b, row(g_sgu[0]),
        w_ba, w_bs, w_o1,
        (w_q_cross[0], w_kv_cross[0], w_o_cross[0], w_gate_up[0], w_down[0]))

    h = _cross_attention(h, mem.reshape(B * mem.shape[1], D), row(g_mem[0]), w_kv,
                         row(g_cross[0]), w_q, w_o2, batch=B)
    out = _ffn(h, row(g_ffn[0]), w_gu, w_dn, row(g_final))
    return out.reshape(B, S, D)
```

```python
import functools
import math

import jax
import jax.numpy as jnp
import numpy as np
from jax import lax
from jax.experimental import pallas as pl
from jax.experimental.pallas import tpu as pltpu

D_MODEL = 1024
HEAD_DIM = 64
DIL_GROUPS = ((128, 1), (512, 4), (2048, 16))
N_GROUPS = 3
HEADS_PER_GROUP = 4
N_ATTN_HEADS = N_GROUPS * HEADS_PER_GROUP
ATTN_W = N_ATTN_HEADS * HEAD_DIM
GROUP_W = HEADS_PER_GROUP * HEAD_DIM
BLK = 128
LANES = 128
BF16_SUBLANES = 16
MAX_SUBLANE_STRIDE = 4
HALVES = GROUP_W // LANES
SGU_CHUNK = 128
SGU_GROUPS = 4
SGU_W = 512
SGU_GROUP_W = SGU_W // SGU_GROUPS
N_BRANCH = 2
MEM_HEADS = 4
MEM_HEAD_DIM = 128
MEM_W = MEM_HEADS * MEM_HEAD_DIM
D_FF = 2816
EPS = 1e-6
LOG2E = math.log2(math.e)

ATTN_TILE = 2048
TM = 1024
TM_CROSS = 2048
SUB_QKV = 512
SUB_MIX = 512
SUB_CROSS = 512
SUB_FFN = 256
VMEM_LIMIT = 48 * 1024 * 1024
VMEM_LIMIT_BIG = 56 * 1024 * 1024
VMEM_LIMIT_CROSS = 40 * 1024 * 1024
VMEM_LIMIT_FFN = 28 * 1024 * 1024

F32 = jnp.float32
BF16 = jnp.bfloat16


def _rms(xf, g):
    r = lax.rsqrt(jnp.mean(xf * xf, axis=-1, keepdims=True) + EPS)
    return xf * r * g


def _alibi_slopes_grouped():
    def pow2(n):
        start = 2.0 ** (-8.0 / n)
        return [start ** (i + 1) for i in range(n)]
    n = N_ATTN_HEADS
    if math.log2(n).is_integer():
        s = pow2(n)
    else:
        c = 2 ** int(math.floor(math.log2(n)))
        s = pow2(c) + pow2(2 * c)[0::2][: n - c]
    s = np.array(sorted(s, reverse=True), dtype=np.float32)
    return s.reshape(N_GROUPS, HEADS_PER_GROUP)


def _attn_bias(gi):
    window, dil = DIL_GROUPS[gi]
    n_back = window // dil
    steps = (np.arange(BLK)[:, None] + BLK) - np.arange(2 * BLK)[None, :]
    band = (steps >= 0) & (steps <= n_back)
    dist = (np.clip(steps, 0, None) * dil).astype(np.float32)
    slopes = _alibi_slopes_grouped()[gi]
    bias = -slopes[:, None, None] * dist[None]
    bias = np.where(band[None], bias.astype(np.float64) * LOG2E, -np.inf).astype(np.float32)
    return bias.reshape(HEADS_PER_GROUP * BLK, 2 * BLK)


def _const_spec(shape):
    nd = len(shape)
    return pl.BlockSpec(shape, lambda *_: (0,) * nd, pipeline_mode=pl.Buffered(1))


def _cast_specs(weights, steps):
    in_specs, out_specs, out_shapes = [], [], []
    for w, col0 in weights:
        rows, cols = w.shape
        assert rows % (steps * BF16_SUBLANES) == 0 and col0 % LANES == 0
        in_specs.append(pl.BlockSpec((rows // steps, cols), lambda i, *_: (i, 0)))
        out_specs.append(pl.BlockSpec((rows // steps, cols - col0), lambda i, *_: (i, 0)))
        out_shapes.append(jax.ShapeDtypeStruct((rows, cols - col0), BF16))
    return in_specs, out_specs, out_shapes


def _cast_slabs(in_refs, out_refs):
    for src, dst in zip(in_refs, out_refs):
        col0 = src.shape[1] - dst.shape[1]
        dst[...] = src[:, col0:].astype(BF16)


def _residue_rows(src, tmp, dil):
    n = src.shape[0]
    if dil <= MAX_SUBLANE_STRIDE:
        for r in range(dil):
            yield r, (src[pl.ds(r, n // dil, stride=dil), :] if dil > 1 else src[...])
        return
    outer = dil // MAX_SUBLANE_STRIDE
    assert outer <= MAX_SUBLANE_STRIDE
    for r0 in range(MAX_SUBLANE_STRIDE):
        tmp[r0] = src[pl.ds(r0, n // MAX_SUBLANE_STRIDE, stride=MAX_SUBLANE_STRIDE), :]
        for r1 in range(outer):
            yield r1 * MAX_SUBLANE_STRIDE + r0, tmp[r0, pl.ds(r1, n // dil, stride=outer), :]


def _qkv_kernel(x_ref, g_ref, wf_ref, *refs, n_cast):
    cast_in, refs = refs[:n_cast], refs[n_cast:]
    out_refs, cast_out = refs[:3 * N_GROUPS], refs[3 * N_GROUPS:3 * N_GROUPS + n_cast]
    w_ref, scr, scr2 = refs[-3:]

    @pl.when(pl.program_id(0) == 0)
    def _():
        w_ref[...] = wf_ref[...].astype(BF16)

    _cast_slabs(cast_in, cast_out)
    for part in range(TM // SUB_QKV):
        rows = slice(part * SUB_QKV, (part + 1) * SUB_QKV)
        a = _rms(x_ref[rows, :], g_ref[...]).astype(BF16)
        res = jnp.dot(a, w_ref[...], preferred_element_type=F32)
        for which in range(3):
            for gi, (_, dil) in enumerate(DIL_GROUPS):
                slab = (which * ATTN_W + gi * GROUP_W) // LANES
                out = out_refs[gi * 3 + which]
                out_rows = slice(part * SUB_QKV // dil, (part + 1) * SUB_QKV // dil)
                for half in range(HALVES):
                    cols = slice((slab + half) * LANES, (slab + half + 1) * LANES)
                    if dil == 1:
                        pieces = [(0, res[:, cols])]
                    else:
                        src = scr.at[part, slab + half]
                        src[...] = res[:, cols]
                        pieces = _residue_rows(src, scr2.at[part, which * HALVES + half], dil)
                    for r, val in pieces:
                        if which == 0:
                            val = val * (HEAD_DIM ** -0.5 * LOG2E)
                        lo = r * GROUP_W + half * LANES
                        out[out_rows, lo:lo + LANES] = val.astype(BF16)


def _qkv_proj(x2, g_mix, w_in, cast_weights):
    T = x2.shape[0]
    steps = T // TM
    out_shapes, out_specs = [], []
    for _, dil in DIL_GROUPS:
        for _ in range(3):
            out_shapes.append(jax.ShapeDtypeStruct((T // dil, dil * GROUP_W), BF16))
            out_specs.append(pl.BlockSpec((TM // dil, dil * GROUP_W), lambda i: (i, 0)))
    cast_in, cast_out, cast_shapes = _cast_specs(cast_weights, steps)
    return pl.pallas_call(
        functools.partial(_qkv_kernel, n_cast=len(cast_weights)),
        grid=(steps,),
        in_specs=[pl.BlockSpec((TM, D_MODEL), lambda i: (i, 0)),
                  _const_spec((1, D_MODEL)),
                  _const_spec((D_MODEL, 3 * ATTN_W))] + cast_in,
        out_specs=out_specs + cast_out,
        out_shape=out_shapes + cast_shapes,
        scratch_shapes=[pltpu.VMEM((D_MODEL, 3 * ATTN_W), BF16),
                        pltpu.VMEM((TM // SUB_QKV, 3 * ATTN_W // LANES, SUB_QKV, LANES), F32),
                        pltpu.VMEM((TM // SUB_QKV, 3 * HALVES, MAX_SUBLANE_STRIDE,
                                    SUB_QKV // MAX_SUBLANE_STRIDE, LANES), F32)],
        compiler_params=pltpu.CompilerParams(
            dimension_semantics=("arbitrary",), vmem_limit_bytes=VMEM_LIMIT_BIG),
        name="qkv_proj",
    )(x2, g_mix, w_in, *[w for w, _ in cast_weights])


def _attn_block(q, k2, v2, bias, no_prev, head_masks, low_lanes):
    zero = jnp.zeros_like(q)
    qs = jnp.concatenate([jnp.where(m, q, zero) for m in head_masks], axis=0)
    s = lax.dot_general(qs, k2, (((1,), (1,)), ((), ())),
                        preferred_element_type=F32) + bias
    if no_prev is not None:
        s = jnp.where(no_prev, -jnp.inf, s)
    mx = jnp.max(s, axis=-1, keepdims=True)
    e = jnp.exp2(s - mx)
    den = jnp.sum(e, axis=-1, keepdims=True)
    pv = jnp.dot(e.astype(BF16), v2, preferred_element_type=F32)

    def head_rows(x, h):
        return x[h * BLK:(h + 1) * BLK]

    acc, m_rep, l_rep = [], [], []
    for half in range(HALVES):
        ha, hb = 2 * half, 2 * half + 1
        cols = slice(half * LANES, (half + 1) * LANES)
        acc.append(jnp.where(low_lanes, head_rows(pv, ha)[:, cols], head_rows(pv, hb)[:, cols]))
        m_rep.append(jnp.where(low_lanes, head_rows(mx, ha), head_rows(mx, hb)))
        l_rep.append(jnp.where(low_lanes, head_rows(den, ha), head_rows(den, hb)))
    return acc, m_rep, l_rep


def _attn_kernel(*refs):
    in_refs, (y_ref, nat, stage) = refs[:6 * N_GROUPS], refs[6 * N_GROUPS:]
    first_tile = pl.program_id(1) == 0
    lane = lax.broadcasted_iota(jnp.int32, (1, GROUP_W), 1)
    head_masks = [(lane >= HEAD_DIM * h) & (lane < HEAD_DIM * (h + 1))
                  for h in range(HEADS_PER_GROUP)]
    low_lanes = lax.broadcasted_iota(jnp.int32, (1, LANES), 1) < HEAD_DIM
    key_col = lax.broadcasted_iota(jnp.int32, (1, 2 * BLK), 1)
    no_prev_first = jnp.logical_and(first_tile, key_col < BLK)

    def group_blocks(gi):
        q_ref, k_ref, kp_ref, v_ref, vp_ref, bias_ref = in_refs[6 * gi:6 * gi + 6]
        dil = DIL_GROUPS[gi][1]
        for r in range(dil):
            lanes = slice(r * GROUP_W, (r + 1) * GROUP_W)
            for i in range(ATTN_TILE // dil // BLK):
                cur = slice(i * BLK, (i + 1) * BLK)
                if i == 0:
                    kp, vp = kp_ref[:, lanes], vp_ref[:, lanes]
                else:
                    prev = slice((i - 1) * BLK, i * BLK)
                    kp, vp = k_ref[prev, lanes], v_ref[prev, lanes]
                k2 = jnp.concatenate([kp, k_ref[cur, lanes]], axis=0)
                v2 = jnp.concatenate([vp, v_ref[cur, lanes]], axis=0)
                yield r, i, _attn_block(q_ref[cur, lanes], k2, v2, bias_ref[...],
                                        no_prev_first if i == 0 else None,
                                        head_masks, low_lanes)

    for gi in range(1, N_GROUPS):
        dil = DIL_GROUPS[gi][1]
        if dil <= MAX_SUBLANE_STRIDE:
            for r, i, stats in group_blocks(gi):
                dst = pl.ds(i * BLK * dil + r, BLK, stride=dil)
                for kind, parts in enumerate(stats):
                    for half in range(HALVES):
                        nat[gi - 1, kind, half, dst, :] = parts[half]
            continue
        outer = dil // MAX_SUBLANE_STRIDE
        for r, i, stats in group_blocks(gi):
            r1, r0 = divmod(r, MAX_SUBLANE_STRIDE)
            dst = pl.ds(i * BLK * outer + r1, BLK, stride=outer)
            for kind, parts in enumerate(stats):
                for half in range(HALVES):
                    stage[kind, half, r0, dst, :] = parts[half]
        for kind in range(3):
            for half in range(HALVES):
                for r0 in range(MAX_SUBLANE_STRIDE):
                    dst = pl.ds(r0, ATTN_TILE // MAX_SUBLANE_STRIDE, stride=MAX_SUBLANE_STRIDE)
                    nat[gi - 1, kind, half, dst, :] = stage[kind, half, r0]

    for r, i, (acc0, m0, l0) in group_blocks(0):
        rows = slice(i * BLK, (i + 1) * BLK)
        for half in range(HALVES):
            accs = [acc0[half]] + [nat[g, 0, half, rows, :] for g in range(N_GROUPS - 1)]
            ms = [m0[half]] + [nat[g, 1, half, rows, :] for g in range(N_GROUPS - 1)]
            ls = [l0[half]] + [nat[g, 2, half, rows, :] for g in range(N_GROUPS - 1)]
            m_all = jnp.maximum(jnp.maximum(ms[0], ms[1]), ms[2])
            ws = [jnp.exp2(m - m_all) for m in ms]
            num = ws[0] * accs[0] + ws[1] * accs[1] + ws[2] * accs[2]
            den = ws[0] * ls[0] + ws[1] * ls[1] + ws[2] * ls[2]
            y_ref[rows, half * LANES:(half + 1) * LANES] = (num * (1.0 / den)).astype(BF16)


def _dilated_attention(qkv, batch):
    T = qkv[0].shape[0]
    tiles_per_batch = T // ATTN_TILE // batch

    def cur_map(b, t):
        return (b * tiles_per_batch + t, 0)

    operands, in_specs = [], []
    for gi, (_, dil) in enumerate(DIL_GROUPS):
        qd, kd, vd = qkv[3 * gi:3 * gi + 3]
        rows, width = ATTN_TILE // dil, dil * GROUP_W
        prev_blocks = rows // BLK

        def prev_map(b, t, prev_blocks=prev_blocks):
            return (jnp.maximum((b * tiles_per_batch + t) * prev_blocks - 1, 0), 0)

        bias = jnp.asarray(_attn_bias(gi))
        operands += [qd, kd, kd, vd, vd, bias]
        in_specs += [pl.BlockSpec((rows, width), cur_map),
                     pl.BlockSpec((rows, width), cur_map),
                     pl.BlockSpec((BLK, width), prev_map),
                     pl.BlockSpec((rows, width), cur_map),
                     pl.BlockSpec((BLK, width), prev_map),
                     _const_spec(bias.shape)]
    return pl.pallas_call(
        _attn_kernel,
        grid=(batch, tiles_per_batch),
        in_specs=in_specs,
        out_specs=pl.BlockSpec((ATTN_TILE, GROUP_W), cur_map),
        out_shape=jax.ShapeDtypeStruct((T, GROUP_W), BF16),
        scratch_shapes=[pltpu.VMEM((N_GROUPS - 1, 3, HALVES, ATTN_TILE, LANES), F32),
                        pltpu.VMEM((3, HALVES, MAX_SUBLANE_STRIDE,
                                    ATTN_TILE // MAX_SUBLANE_STRIDE, LANES), F32)],
        compiler_params=pltpu.CompilerParams(
            dimension_semantics=("arbitrary", "arbitrary"), vmem_limit_bytes=VMEM_LIMIT_BIG),
        name="dilated_attn",
    )(*operands)


def _mixer_kernel(x_ref, y_ref, g_ref, w_ref, bg_ref, ws_ref, bs_ref, gs_ref,
                  wba_ref, wbs_ref, wo_ref, *refs, n_cast):
    cast_in, h_ref, cast_out = refs[:n_cast], refs[n_cast], refs[n_cast + 1:]
    _cast_slabs(cast_in, cast_out)
    row = lax.broadcasted_iota(jnp.int32, (SGU_CHUNK, SGU_CHUNK), 0)
    col = lax.broadcasted_iota(jnp.int32, (SGU_CHUNK, SGU_CHUNK), 1)
    ws = [jnp.where(row >= col, ws_ref[g], 0.0).astype(BF16) for g in range(SGU_GROUPS)]
    bias = jnp.concatenate(
        [jnp.broadcast_to(jnp.sum(jnp.where(row == col, bs_ref[g:g + 1, :], 0.0),
                                  axis=1, keepdims=True), (SGU_CHUNK, SGU_GROUP_W))
         for g in range(SGU_GROUPS)], axis=1)
    for part in range(TM // SUB_MIX):
        rows = slice(part * SUB_MIX, (part + 1) * SUB_MIX)
        x = x_ref[rows, :]
        a = _rms(x, g_ref[...]).astype(BF16)
        proj = jnp.dot(a, w_ref[...], preferred_element_type=F32)

        z = jax.nn.gelu(proj[:, :2 * SGU_W], approximate=True)
        u = z[:, :SGU_W]
        v = _rms(z[:, SGU_W:], gs_ref[...]).astype(BF16)
        n_chunks = SUB_MIX // SGU_CHUNK
        mixed = []
        for g in range(SGU_GROUPS):
            gl = slice(g * SGU_GROUP_W, (g + 1) * SGU_GROUP_W)
            vg = jnp.concatenate([v[c * SGU_CHUNK:(c + 1) * SGU_CHUNK, gl]
                                  for c in range(n_chunks)], axis=1)
            mixed.append(jnp.dot(ws[g], vg, preferred_element_type=F32))
        chunks = [jnp.concatenate([m[:, c * SGU_GROUP_W:(c + 1) * SGU_GROUP_W] for m in mixed],
                                  axis=1) + bias for c in range(n_chunks)]
        y_sgu = u * jnp.concatenate(chunks, axis=0)

        gates = 1.0 / (1.0 + jnp.exp(-(proj[:, 2 * SGU_W:] + bg_ref[...])))
        merged = (gates[:, :D_MODEL] * jnp.dot(y_ref[rows, :], wba_ref[...],
                                                preferred_element_type=F32)
                  + gates[:, D_MODEL:] * jnp.dot(y_sgu.astype(BF16), wbs_ref[...],
                                                 preferred_element_type=F32))
        h_ref[rows, :] = x + jnp.dot(merged.astype(BF16), wo_ref[...],
                                     preferred_element_type=F32)


def _mixer(x2, y_attn, g_mix, w_uvgl, b_gate, w_s, b_s, g_sgu, w_ba, w_bs, w_out, cast_weights):
    T = x2.shape[0]
    steps = T // TM
    row_spec = lambda w: pl.BlockSpec((TM, w), lambda i: (i, 0))
    consts = (g_mix, w_uvgl, b_gate, w_s, b_s, g_sgu, w_ba, w_bs, w_out)
    cast_in, cast_out, cast_shapes = _cast_specs(cast_weights, steps)
    return pl.pallas_call(
        functools.partial(_mixer_kernel, n_cast=len(cast_weights)),
        grid=(steps,),
        in_specs=([row_spec(D_MODEL), row_spec(GROUP_W)]
                  + [_const_spec(c.shape) for c in consts] + cast_in),
        out_specs=[row_spec(D_MODEL)] + cast_out,
        out_shape=[jax.ShapeDtypeStruct((T, D_MODEL), F32)] + cast_shapes,
        compiler_params=pltpu.CompilerParams(
            dimension_semantics=("arbitrary",), vmem_limit_bytes=VMEM_LIMIT_BIG),
        name="mixer_tail",
    )(x2, y_attn, *consts, *[w for w, _ in cast_weights])


def _cross_kernel(h_ref, mem_ref, gm_ref, wkv_ref, g_ref, wq_ref, wo_ref, out_ref, kv_ref):
    @pl.when(pl.program_id(1) == 0)
    def _():
        m = _rms(mem_ref[...], gm_ref[...]).astype(BF16)
        kv_ref[...] = jnp.dot(m, wkv_ref[...], preferred_element_type=F32).astype(BF16)

    for part in range(TM_CROSS // SUB_CROSS):
        rows = slice(part * SUB_CROSS, (part + 1) * SUB_CROSS)
        h = h_ref[rows, :]
        c = _rms(h, g_ref[...]).astype(BF16)
        q = jnp.dot(c, wq_ref[...], preferred_element_type=F32).astype(BF16)
        heads = []
        for hd in range(MEM_HEADS):
            cols = slice(hd * MEM_HEAD_DIM, (hd + 1) * MEM_HEAD_DIM)
            k = kv_ref[:, cols]
            v = kv_ref[:, MEM_W + hd * MEM_HEAD_DIM:MEM_W + (hd + 1) * MEM_HEAD_DIM]
            s = lax.dot_general(q[:, cols], k, (((1,), (1,)), ((), ())),
                                preferred_element_type=F32) * (MEM_HEAD_DIM ** -0.5)
            e = jnp.exp(s - jnp.max(s, axis=-1, keepdims=True))
            den = jnp.sum(e, axis=-1, keepdims=True)
            o = jnp.dot(e.astype(BF16), v, preferred_element_type=F32) * (1.0 / den)
            heads.append(o.astype(BF16))
        o = jnp.concatenate(heads, axis=1)
        out_ref[rows, :] = h + jnp.dot(o, wo_ref[...], preferred_element_type=F32)


def _cross_attention(h, mem2, g_mem, w_kv, g_cross, w_q, w_o, batch):
    T = h.shape[0]
    blocks_per_batch = T // batch // TM_CROSS
    mem_len = mem2.shape[0] // batch
    consts = (g_mem, w_kv, g_cross, w_q, w_o)
    row_map = lambda b, i: (b * blocks_per_batch + i, 0)
    return pl.pallas_call(
        _cross_kernel,
        grid=(batch, blocks_per_batch),
        in_specs=[pl.BlockSpec((TM_CROSS, D_MODEL), row_map),
                  pl.BlockSpec((mem_len, D_MODEL), lambda b, i: (b, 0))]
                 + [_const_spec(c.shape) for c in consts],
        out_specs=pl.BlockSpec((TM_CROSS, D_MODEL), row_map),
        out_shape=jax.ShapeDtypeStruct((T, D_MODEL), F32),
        scratch_shapes=[pltpu.VMEM((mem_len, 2 * MEM_W), BF16)],
        compiler_params=pltpu.CompilerParams(
            dimension_semantics=("arbitrary", "arbitrary"), vmem_limit_bytes=VMEM_LIMIT_CROSS),
        name="cross_attn",
    )(h, mem2, *consts)


def _ffn_kernel(h_ref, g_ref, wgu_ref, wd_ref, gf_ref, out_ref):
    for part in range(TM // SUB_FFN):
        rows = slice(part * SUB_FFN, (part + 1) * SUB_FFN)
        h = h_ref[rows, :]
        f = _rms(h, g_ref[...]).astype(BF16)
        gu = jnp.dot(f, wgu_ref[...], preferred_element_type=F32)
        gt, up = gu[:, :D_FF], gu[:, D_FF:]
        act = (gt * (1.0 / (1.0 + jnp.exp(-gt))) * up).astype(BF16)
        acc = h + jnp.dot(act, wd_ref[...], preferred_element_type=F32)
        out_ref[rows, :] = _rms(acc, gf_ref[...])


def _ffn(h, g_ffn, w_gu, w_down, g_final):
    T = h.shape[0]
    consts = (g_ffn, w_gu, w_down, g_final)
    return pl.pallas_call(
        _ffn_kernel,
        grid=(T // TM,),
        in_specs=[pl.BlockSpec((TM, D_MODEL), lambda i: (i, 0)),
                  _const_spec(g_ffn.shape),
                  pl.BlockSpec(memory_space=pltpu.VMEM),
                  pl.BlockSpec(memory_space=pltpu.VMEM),
                  _const_spec(g_final.shape)],
        out_specs=pl.BlockSpec((TM, D_MODEL), lambda i: (i, 0)),
        out_shape=jax.ShapeDtypeStruct((T, D_MODEL), F32),
        compiler_params=pltpu.CompilerParams(
            dimension_semantics=("arbitrary",), vmem_limit_bytes=VMEM_LIMIT_FFN),
        name="ffn",
    )(h, *consts)


def kernel(x, mem, g_mix, w_in, b_gate, w_sgu_spatial, b_sgu_spatial, g_sgu, w_branch_attn,
           w_branch_sgu, w_out, g_cross, g_mem, w_q_cross, w_kv_cross, w_o_cross, g_ffn,
           w_gate_up, w_down, g_final):
    B, S, D = x.shape
    assert D == D_MODEL and S % ATTN_TILE == 0 and w_in.shape[0] == 1
    T = B * S
    x2 = x.reshape(T, D)
    row = lambda v: v.reshape(1, -1)

    g_mix2 = row(g_mix[0])
    w_in2 = w_in.reshape(D, w_in.shape[-1])

    *qkv, w_uvgl, w_ba, w_bs, w_o1 = _qkv_proj(
        x2, g_mix2, w_in2,
        ((w_in2, 3 * ATTN_W), (w_branch_attn[0], 0), (w_branch_sgu[0], 0), (w_out[0], 0)))
    y_attn = _dilated_attention(qkv, batch=B)

    h, w_q, w_kv, w_o2, w_gu, w_dn = _mixer(
        x2, y_attn, g_mix2, w_uvgl, row(b_gate[0]), w_sgu_spatial[0], b_sgu_spatial[0],
        row(g_sgu[0]), w_ba, w_bs, w_o1,
        ((w_q_cross[0], 0), (w_kv_cross[0], 0), (w_o_cross[0], 0), (w_gate_up[0], 0),
         (w_down[0], 0)))

    h = _cross_attention(h, mem.reshape(B * mem.shape[1], D), row(g_mem[0]), w_kv,
                         row(g_cross[0]), w_q, w_o2, batch=B)
    out = _ffn(h, row(g_ffn[0]), w_gu, w_dn, row(g_final))
    return out.reshape(B, S, D)
```

```python
import functools
import math

import jax
import jax.numpy as jnp
import numpy as np
from jax import lax
from jax.experimental import pallas as pl
from jax.experimental.pallas import tpu as pltpu

D_MODEL = 1024
HEAD_DIM = 64
DIL_GROUPS = ((128, 1), (512, 4), (2048, 16))
N_GROUPS = 3
HEADS_PER_GROUP = 4
N_ATTN_HEADS = N_GROUPS * HEADS_PER_GROUP
ATTN_W = N_ATTN_HEADS * HEAD_DIM
GROUP_W = HEADS_PER_GROUP * HEAD_DIM
BLK = 128
LANES = 128
BF16_SUBLANES = 16
MAX_SUBLANE_STRIDE = 4
HALVES = GROUP_W // LANES
SGU_CHUNK = 128
SGU_GROUPS = 4
SGU_W = 512
SGU_GROUP_W = SGU_W // SGU_GROUPS
N_BRANCH = 2
MEM_HEADS = 4
MEM_HEAD_DIM = 128
MEM_W = MEM_HEADS * MEM_HEAD_DIM
D_FF = 2816
EPS = 1e-6
LOG2E = math.log2(math.e)

ATTN_TILE = 2048
TM = 1024
TM_CROSS = 2048
SUB_QKV = 512
SUB_MIX = 512
SUB_CROSS = 512
SUB_FFN = 256
VMEM_LIMIT = 48 * 1024 * 1024
VMEM_LIMIT_BIG = 56 * 1024 * 1024
VMEM_LIMIT_ATTN = 50 * 1024 * 1024
VMEM_LIMIT_MIX = 40 * 1024 * 1024
VMEM_LIMIT_CROSS = 40 * 1024 * 1024
VMEM_LIMIT_FFN = 28 * 1024 * 1024

F32 = jnp.float32
BF16 = jnp.bfloat16


def _rms(xf, g):
    r = lax.rsqrt(jnp.mean(xf * xf, axis=-1, keepdims=True) + EPS)
    return xf * r * g


def _alibi_slopes_grouped():
    def pow2(n):
        start = 2.0 ** (-8.0 / n)
        return [start ** (i + 1) for i in range(n)]
    n = N_ATTN_HEADS
    if math.log2(n).is_integer():
        s = pow2(n)
    else:
        c = 2 ** int(math.floor(math.log2(n)))
        s = pow2(c) + pow2(2 * c)[0::2][: n - c]
    s = np.array(sorted(s, reverse=True), dtype=np.float32)
    return s.reshape(N_GROUPS, HEADS_PER_GROUP)


def _attn_bias(gi):
    window, dil = DIL_GROUPS[gi]
    n_back = window // dil
    steps = (np.arange(BLK)[:, None] + BLK) - np.arange(2 * BLK)[None, :]
    band = (steps >= 0) & (steps <= n_back)
    dist = (np.clip(steps, 0, None) * dil).astype(np.float32)
    slopes = _alibi_slopes_grouped()[gi]
    bias = -slopes[:, None, None] * dist[None]
    bias = np.where(band[None], bias.astype(np.float64) * LOG2E, -np.inf).astype(np.float32)
    return bias.reshape(HEADS_PER_GROUP * BLK, 2 * BLK)


def _const_spec(shape):
    nd = len(shape)
    return pl.BlockSpec(shape, lambda *_: (0,) * nd, pipeline_mode=pl.Buffered(1))


def _cast_specs(weights, steps):
    in_specs, out_specs, out_shapes = [], [], []
    for w, col0 in weights:
        rows, cols = w.shape
        assert rows % (steps * BF16_SUBLANES) == 0 and col0 % LANES == 0
        in_specs.append(pl.BlockSpec((rows // steps, cols), lambda i, *_: (i, 0)))
        out_specs.append(pl.BlockSpec((rows // steps, cols - col0), lambda i, *_: (i, 0)))
        out_shapes.append(jax.ShapeDtypeStruct((rows, cols - col0), BF16))
    return in_specs, out_specs, out_shapes


def _cast_slabs(in_refs, out_refs):
    for src, dst in zip(in_refs, out_refs):
        col0 = src.shape[1] - dst.shape[1]
        dst[...] = src[:, col0:].astype(BF16)


def _residue_rows(src, tmp, dil):
    n = src.shape[0]
    if dil <= MAX_SUBLANE_STRIDE:
        for r in range(dil):
            yield r, (src[pl.ds(r, n // dil, stride=dil), :] if dil > 1 else src[...])
        return
    outer = dil // MAX_SUBLANE_STRIDE
    assert outer <= MAX_SUBLANE_STRIDE
    for r0 in range(MAX_SUBLANE_STRIDE):
        tmp[r0] = src[pl.ds(r0, n // MAX_SUBLANE_STRIDE, stride=MAX_SUBLANE_STRIDE), :]
        for r1 in range(outer):
            yield r1 * MAX_SUBLANE_STRIDE + r0, tmp[r0, pl.ds(r1, n // dil, stride=outer), :]


def _qkv_kernel(x_ref, g_ref, wf_ref, *refs, n_cast):
    cast_in, refs = refs[:n_cast], refs[n_cast:]
    out_refs, cast_out = refs[:3 * N_GROUPS], refs[3 * N_GROUPS:3 * N_GROUPS + n_cast]
    w_ref, scr, scr2 = refs[-3:]

    @pl.when(pl.program_id(0) == 0)
    def _():
        w_ref[...] = wf_ref[...].astype(BF16)

    _cast_slabs(cast_in, cast_out)
    for part in range(TM // SUB_QKV):
        rows = slice(part * SUB_QKV, (part + 1) * SUB_QKV)
        a = _rms(x_ref[rows, :], g_ref[...]).astype(BF16)
        res = jnp.dot(a, w_ref[...], preferred_element_type=F32)
        for which in range(3):
            for gi, (_, dil) in enumerate(DIL_GROUPS):
                slab = (which * ATTN_W + gi * GROUP_W) // LANES
                out = out_refs[gi * 3 + which]
                out_rows = slice(part * SUB_QKV // dil, (part + 1) * SUB_QKV // dil)
                for half in range(HALVES):
                    cols = slice((slab + half) * LANES, (slab + half + 1) * LANES)
                    if dil == 1:
                        pieces = [(0, res[:, cols])]
                    else:
                        src = scr.at[part, slab + half]
                        src[...] = res[:, cols]
                        pieces = _residue_rows(src, scr2.at[part, which * HALVES + half], dil)
                    for r, val in pieces:
                        if which == 0:
                            val = val * (HEAD_DIM ** -0.5 * LOG2E)
                        lo = r * GROUP_W + half * LANES
                        out[out_rows, lo:lo + LANES] = val.astype(BF16)


def _qkv_proj(x2, g_mix, w_in, cast_weights):
    T = x2.shape[0]
    steps = T // TM
    out_shapes, out_specs = [], []
    for _, dil in DIL_GROUPS:
        for _ in range(3):
            out_shapes.append(jax.ShapeDtypeStruct((T // dil, dil * GROUP_W), BF16))
            out_specs.append(pl.BlockSpec((TM // dil, dil * GROUP_W), lambda i: (i, 0)))
    cast_in, cast_out, cast_shapes = _cast_specs(cast_weights, steps)
    return pl.pallas_call(
        functools.partial(_qkv_kernel, n_cast=len(cast_weights)),
        grid=(steps,),
        in_specs=[pl.BlockSpec((TM, D_MODEL), lambda i: (i, 0)),
                  _const_spec((1, D_MODEL)),
                  _const_spec((D_MODEL, 3 * ATTN_W))] + cast_in,
        out_specs=out_specs + cast_out,
        out_shape=out_shapes + cast_shapes,
        scratch_shapes=[pltpu.VMEM((D_MODEL, 3 * ATTN_W), BF16),
                        pltpu.VMEM((TM // SUB_QKV, 3 * ATTN_W // LANES, SUB_QKV, LANES), F32),
                        pltpu.VMEM((TM // SUB_QKV, 3 * HALVES, MAX_SUBLANE_STRIDE,
                                    SUB_QKV // MAX_SUBLANE_STRIDE, LANES), F32)],
        compiler_params=pltpu.CompilerParams(
            dimension_semantics=("arbitrary",), vmem_limit_bytes=VMEM_LIMIT_BIG),
        name="qkv_proj",
    )(x2, g_mix, w_in, *[w for w, _ in cast_weights])


def _attn_block(q, k2, v2, bias, no_prev, head_masks, low_lanes):
    zero = jnp.zeros_like(q)
    qs = jnp.concatenate([jnp.where(m, q, zero) for m in head_masks], axis=0)
    s = lax.dot_general(qs, k2, (((1,), (1,)), ((), ())),
                        preferred_element_type=F32) + bias
    if no_prev is not None:
        s = jnp.where(no_prev, -jnp.inf, s)
    mx = jnp.max(s, axis=-1, keepdims=True)
    e = jnp.exp2(s - mx)
    den = jnp.sum(e, axis=-1, keepdims=True)
    pv = jnp.dot(e.astype(BF16), v2, preferred_element_type=F32)

    def head_rows(x, h):
        return x[h * BLK:(h + 1) * BLK]

    acc, m_rep, l_rep = [], [], []
    for half in range(HALVES):
        ha, hb = 2 * half, 2 * half + 1
        cols = slice(half * LANES, (half + 1) * LANES)
        acc.append(jnp.where(low_lanes, head_rows(pv, ha)[:, cols], head_rows(pv, hb)[:, cols]))
        m_rep.append(jnp.where(low_lanes, head_rows(mx, ha), head_rows(mx, hb)))
        l_rep.append(jnp.where(low_lanes, head_rows(den, ha), head_rows(den, hb)))
    return acc, m_rep, l_rep


def _attn_kernel(*refs):
    in_refs, (y_ref, nat, stage) = refs[:6 * N_GROUPS], refs[6 * N_GROUPS:]
    first_tile = pl.program_id(1) == 0
    lane = lax.broadcasted_iota(jnp.int32, (1, GROUP_W), 1)
    head_masks = [(lane >= HEAD_DIM * h) & (lane < HEAD_DIM * (h + 1))
                  for h in range(HEADS_PER_GROUP)]
    low_lanes = lax.broadcasted_iota(jnp.int32, (1, LANES), 1) < HEAD_DIM
    key_col = lax.broadcasted_iota(jnp.int32, (1, 2 * BLK), 1)
    no_prev_first = jnp.logical_and(first_tile, key_col < BLK)

    def group_blocks(gi):
        q_ref, k_ref, kp_ref, v_ref, vp_ref, bias_ref = in_refs[6 * gi:6 * gi + 6]
        dil = DIL_GROUPS[gi][1]
        for r in range(dil):
            lanes = slice(r * GROUP_W, (r + 1) * GROUP_W)
            for i in range(ATTN_TILE // dil // BLK):
                cur = slice(i * BLK, (i + 1) * BLK)
                if i == 0:
                    kp, vp = kp_ref[:, lanes], vp_ref[:, lanes]
                else:
                    prev = slice((i - 1) * BLK, i * BLK)
                    kp, vp = k_ref[prev, lanes], v_ref[prev, lanes]
                k2 = jnp.concatenate([kp, k_ref[cur, lanes]], axis=0)
                v2 = jnp.concatenate([vp, v_ref[cur, lanes]], axis=0)
                yield r, i, _attn_block(q_ref[cur, lanes], k2, v2, bias_ref[...],
                                        no_prev_first if i == 0 else None,
                                        head_masks, low_lanes)

    for gi in range(1, N_GROUPS):
        dil = DIL_GROUPS[gi][1]
        if dil <= MAX_SUBLANE_STRIDE:
            for r, i, stats in group_blocks(gi):
                dst = pl.ds(i * BLK * dil + r, BLK, stride=dil)
                for kind, parts in enumerate(stats):
                    for half in range(HALVES):
                        nat[gi - 1, kind, half, dst, :] = parts[half]
            continue
        outer = dil // MAX_SUBLANE_STRIDE
        for r, i, stats in group_blocks(gi):
            r1, r0 = divmod(r, MAX_SUBLANE_STRIDE)
            dst = pl.ds(i * BLK * outer + r1, BLK, stride=outer)
            for kind, parts in enumerate(stats):
                for half in range(HALVES):
                    stage[kind, half, r0, dst, :] = parts[half]
        for kind in range(3):
            for half in range(HALVES):
                for r0 in range(MAX_SUBLANE_STRIDE):
                    dst = pl.ds(r0, ATTN_TILE // MAX_SUBLANE_STRIDE, stride=MAX_SUBLANE_STRIDE)
                    nat[gi - 1, kind, half, dst, :] = stage[kind, half, r0]

    for r, i, (acc0, m0, l0) in group_blocks(0):
        rows = slice(i * BLK, (i + 1) * BLK)
        for half in range(HALVES):
            accs = [acc0[half]] + [nat[g, 0, half, rows, :] for g in range(N_GROUPS - 1)]
            ms = [m0[half]] + [nat[g, 1, half, rows, :] for g in range(N_GROUPS - 1)]
            ls = [l0[half]] + [nat[g, 2, half, rows, :] for g in range(N_GROUPS - 1)]
            m_all = jnp.maximum(jnp.maximum(ms[0], ms[1]), ms[2])
            ws = [jnp.exp2(m - m_all) for m in ms]
            num = ws[0] * accs[0] + ws[1] * accs[1] + ws[2] * accs[2]
            den = ws[0] * ls[0] + ws[1] * ls[1] + ws[2] * ls[2]
            y_ref[rows, half * LANES:(half + 1) * LANES] = (num * (1.0 / den)).astype(BF16)


def _dilated_attention(qkv, batch):
    T = qkv[0].shape[0]
    tiles_per_batch = T // ATTN_TILE // batch

    def cur_map(b, t):
        return (b * tiles_per_batch + t, 0)

    operands, in_specs = [], []
    for gi, (_, dil) in enumerate(DIL_GROUPS):
        qd, kd, vd = qkv[3 * gi:3 * gi + 3]
        rows, width = ATTN_TILE // dil, dil * GROUP_W
        prev_blocks = rows // BLK

        def prev_map(b, t, prev_blocks=prev_blocks):
            return (jnp.maximum((b * tiles_per_batch + t) * prev_blocks - 1, 0), 0)

        bias = jnp.asarray(_attn_bias(gi))
        operands += [qd, kd, kd, vd, vd, bias]
        in_specs += [pl.BlockSpec((rows, width), cur_map),
                     pl.BlockSpec((rows, width), cur_map),
                     pl.BlockSpec((BLK, width), prev_map),
                     pl.BlockSpec((rows, width), cur_map),
                     pl.BlockSpec((BLK, width), prev_map),
                     _const_spec(bias.shape)]
    return pl.pallas_call(
        _attn_kernel,
        grid=(batch, tiles_per_batch),
        in_specs=in_specs,
        out_specs=pl.BlockSpec((ATTN_TILE, GROUP_W), cur_map),
        out_shape=jax.ShapeDtypeStruct((T, GROUP_W), BF16),
        scratch_shapes=[pltpu.VMEM((N_GROUPS - 1, 3, HALVES, ATTN_TILE, LANES), F32),
                        pltpu.VMEM((3, HALVES, MAX_SUBLANE_STRIDE,
                                    ATTN_TILE // MAX_SUBLANE_STRIDE, LANES), F32)],
        compiler_params=pltpu.CompilerParams(
            dimension_semantics=("arbitrary", "arbitrary"), vmem_limit_bytes=VMEM_LIMIT_ATTN),
        name="dilated_attn",
    )(*operands)


def _mixer_kernel(x_ref, y_ref, g_ref, w_ref, bg_ref, ws_ref, bs_ref, gs_ref,
                  wba_ref, wbs_ref, wo_ref, *refs, n_cast):
    cast_in, h_ref, cast_out = refs[:n_cast], refs[n_cast], refs[n_cast + 1:]
    _cast_slabs(cast_in, cast_out)
    row = lax.broadcasted_iota(jnp.int32, (SGU_CHUNK, SGU_CHUNK), 0)
    col = lax.broadcasted_iota(jnp.int32, (SGU_CHUNK, SGU_CHUNK), 1)
    ws = [jnp.where(row >= col, ws_ref[g], 0.0).astype(BF16) for g in range(SGU_GROUPS)]
    bias = jnp.concatenate(
        [jnp.broadcast_to(jnp.sum(jnp.where(row == col, bs_ref[g:g + 1, :], 0.0),
                                  axis=1, keepdims=True), (SGU_CHUNK, SGU_GROUP_W))
         for g in range(SGU_GROUPS)], axis=1)
    for part in range(TM // SUB_MIX):
        rows = slice(part * SUB_MIX, (part + 1) * SUB_MIX)
        x = x_ref[rows, :]
        a = _rms(x, g_ref[...]).astype(BF16)
        proj = jnp.dot(a, w_ref[...], preferred_element_type=F32)

        z = jax.nn.gelu(proj[:, :2 * SGU_W], approximate=True)
        u = z[:, :SGU_W]
        v = _rms(z[:, SGU_W:], gs_ref[...]).astype(BF16)
        n_chunks = SUB_MIX // SGU_CHUNK
        mixed = []
        for g in range(SGU_GROUPS):
            gl = slice(g * SGU_GROUP_W, (g + 1) * SGU_GROUP_W)
            vg = jnp.concatenate([v[c * SGU_CHUNK:(c + 1) * SGU_CHUNK, gl]
                                  for c in range(n_chunks)], axis=1)
            mixed.append(jnp.dot(ws[g], vg, preferred_element_type=F32))
        chunks = [jnp.concatenate([m[:, c * SGU_GROUP_W:(c + 1) * SGU_GROUP_W] for m in mixed],
                                  axis=1) + bias for c in range(n_chunks)]
        y_sgu = u * jnp.concatenate(chunks, axis=0)

        gates = 1.0 / (1.0 + jnp.exp(-(proj[:, 2 * SGU_W:] + bg_ref[...])))
        merged = (gates[:, :D_MODEL] * jnp.dot(y_ref[rows, :], wba_ref[...],
                                                preferred_element_type=F32)
                  + gates[:, D_MODEL:] * jnp.dot(y_sgu.astype(BF16), wbs_ref[...],
                                                 preferred_element_type=F32))
        h_ref[rows, :] = x + jnp.dot(merged.astype(BF16), wo_ref[...],
                                     preferred_element_type=F32)


def _mixer(x2, y_attn, g_mix, w_uvgl, b_gate, w_s, b_s, g_sgu, w_ba, w_bs, w_out, cast_weights):
    T = x2.shape[0]
    steps = T // TM
    row_spec = lambda w: pl.BlockSpec((TM, w), lambda i: (i, 0))
    consts = (g_mix, w_uvgl, b_gate, w_s, b_s, g_sgu, w_ba, w_bs, w_out)
    const_specs = [pl.BlockSpec(memory_space=pltpu.VMEM) if c.dtype == BF16
                   else _const_spec(c.shape) for c in consts]
    cast_in, cast_out, cast_shapes = _cast_specs(cast_weights, steps)
    return pl.pallas_call(
        functools.partial(_mixer_kernel, n_cast=len(cast_weights)),
        grid=(steps,),
        in_specs=[row_spec(D_MODEL), row_spec(GROUP_W)] + const_specs + cast_in,
        out_specs=[row_spec(D_MODEL)] + cast_out,
        out_shape=[jax.ShapeDtypeStruct((T, D_MODEL), F32)] + cast_shapes,
        compiler_params=pltpu.CompilerParams(
            dimension_semantics=("arbitrary",), vmem_limit_bytes=VMEM_LIMIT_MIX),
        name="mixer_tail",
    )(x2, y_attn, *consts, *[w for w, _ in cast_weights])


def _cross_kernel(h_ref, mem_ref, gm_ref, wkv_ref, g_ref, wq_ref, wo_ref, out_ref, kv_ref):
    @pl.when(pl.program_id(1) == 0)
    def _():
        m = _rms(mem_ref[...], gm_ref[...]).astype(BF16)
        kv_ref[...] = jnp.dot(m, wkv_ref[...], preferred_element_type=F32).astype(BF16)

    for part in range(TM_CROSS // SUB_CROSS):
        rows = slice(part * SUB_CROSS, (part + 1) * SUB_CROSS)
        h = h_ref[rows, :]
        c = _rms(h, g_ref[...]).astype(BF16)
        q = jnp.dot(c, wq_ref[...], preferred_element_type=F32).astype(BF16)
        heads = []
        for hd in range(MEM_HEADS):
            cols = slice(hd * MEM_HEAD_DIM, (hd + 1) * MEM_HEAD_DIM)
            k = kv_ref[:, cols]
            v = kv_ref[:, MEM_W + hd * MEM_HEAD_DIM:MEM_W + (hd + 1) * MEM_HEAD_DIM]
            s = lax.dot_general(q[:, cols], k, (((1,), (1,)), ((), ())),
                                preferred_element_type=F32) * (MEM_HEAD_DIM ** -0.5)
            e = jnp.exp(s - jnp.max(s, axis=-1, keepdims=True))
            den = jnp.sum(e, axis=-1, keepdims=True)
            o = jnp.dot(e.astype(BF16), v, preferred_element_type=F32) * (1.0 / den)
            heads.append(o.astype(BF16))
        o = jnp.concatenate(heads, axis=1)
        out_ref[rows, :] = h + jnp.dot(o, wo_ref[...], preferred_element_type=F32)


def _cross_attention(h, mem2, g_mem, w_kv, g_cross, w_q, w_o, batch):
    T = h.shape[0]
    blocks_per_batch = T // batch // TM_CROSS
    mem_len = mem2.shape[0] // batch
    consts = (g_mem, w_kv, g_cross, w_q, w_o)
    row_map = lambda b, i: (b * blocks_per_batch + i, 0)
    return pl.pallas_call(
        _cross_kernel,
        grid=(batch, blocks_per_batch),
        in_specs=[pl.BlockSpec((TM_CROSS, D_MODEL), row_map),
                  pl.BlockSpec((mem_len, D_MODEL), lambda b, i: (b, 0))]
                 + [_const_spec(c.shape) for c in consts],
        out_specs=pl.BlockSpec((TM_CROSS, D_MODEL), row_map),
        out_shape=jax.ShapeDtypeStruct((T, D_MODEL), F32),
        scratch_shapes=[pltpu.VMEM((mem_len, 2 * MEM_W), BF16)],
        compiler_params=pltpu.CompilerParams(
            dimension_semantics=("arbitrary", "arbitrary"), vmem_limit_bytes=VMEM_LIMIT_CROSS),
        name="cross_attn",
    )(h, mem2, *consts)


def _ffn_kernel(h_ref, g_ref, wgu_ref, wd_ref, gf_ref, out_ref):
    for part in range(TM // SUB_FFN):
        rows = slice(part * SUB_FFN, (part + 1) * SUB_FFN)
        h = h_ref[rows, :]
        f = _rms(h, g_ref[...]).astype(BF16)
        gu = jnp.dot(f, wgu_ref[...], preferred_element_type=F32)
        gt, up = gu[:, :D_FF], gu[:, D_FF:]
        act = (gt * (1.0 / (1.0 + jnp.exp(-gt))) * up).astype(BF16)
        acc = h + jnp.dot(act, wd_ref[...], preferred_element_type=F32)
        out_ref[rows, :] = _rms(acc, gf_ref[...])


def _ffn(h, g_ffn, w_gu, w_down, g_final):
    T = h.shape[0]
    consts = (g_ffn, w_gu, w_down, g_final)
    return pl.pallas_call(
        _ffn_kernel,
        grid=(T // TM,),
        in_specs=[pl.BlockSpec((TM, D_MODEL), lambda i: (i, 0)),
                  _const_spec(g_ffn.shape),
                  pl.BlockSpec(memory_space=pltpu.VMEM),
                  pl.BlockSpec(memory_space=pltpu.VMEM),
                  _const_spec(g_final.shape)],
        out_specs=pl.BlockSpec((TM, D_MODEL), lambda i: (i, 0)),
        out_shape=jax.ShapeDtypeStruct((T, D_MODEL), F32),
        compiler_params=pltpu.CompilerParams(
            dimension_semantics=("arbitrary",), vmem_limit_bytes=VMEM_LIMIT_FFN),
        name="ffn",
    )(h, *consts)


def kernel(x, mem, g_mix, w_in, b_gate, w_sgu_spatial, b_sgu_spatial, g_sgu, w_branch_attn,
           w_branch_sgu, w_out, g_cross, g_mem, w_q_cross, w_kv_cross, w_o_cross, g_ffn,
           w_gate_up, w_down, g_final):
    B, S, D = x.shape
    assert D == D_MODEL and S % ATTN_TILE == 0 and w_in.shape[0] == 1
    T = B * S
    x2 = x.reshape(T, D)
    row = lambda v: v.reshape(1, -1)

    g_mix2 = row(g_mix[0])
    w_in2 = w_in.reshape(D, w_in.shape[-1])

    *qkv, w_uvgl, w_ba, w_bs, w_o1, w_q, w_kv, w_o2 = _qkv_proj(
        x2, g_mix2, w_in2,
        ((w_in2, 3 * ATTN_W), (w_branch_attn[0], 0), (w_branch_sgu[0], 0), (w_out[0], 0),
         (w_q_cross[0], 0), (w_kv_cross[0], 0), (w_o_cross[0], 0)))
    y_attn = _dilated_attention(qkv, batch=B)

    h, w_gu, w_dn = _mixer(
        x2, y_attn, g_mix2, w_uvgl, row(b_gate[0]), w_sgu_spatial[0], b_sgu_spatial[0],
        row(g_sgu[0]), w_ba, w_bs, w_o1, ((w_gate_up[0], 0), (w_down[0], 0)))

    h = _cross_attention(h, mem.reshape(B * mem.shape[1], D), row(g_mem[0]), w_kv,
                         row(g_cross[0]), w_q, w_o2, batch=B)
    out = _ffn(h, row(g_ffn[0]), w_gu, w_dn, row(g_final))
    return out.reshape(B, S, D)
```

```python
import functools
import math

import jax
import jax.numpy as jnp
import numpy as np
from jax import lax
from jax.experimental import pallas as pl
from jax.experimental.pallas import tpu as pltpu

D_MODEL = 1024
HEAD_DIM = 64
DIL_GROUPS = ((128, 1), (512, 4), (2048, 16))
N_GROUPS = 3
HEADS_PER_GROUP = 4
N_ATTN_HEADS = N_GROUPS * HEADS_PER_GROUP
ATTN_W = N_ATTN_HEADS * HEAD_DIM
GROUP_W = HEADS_PER_GROUP * HEAD_DIM
BLK = 128
LANES = 128
BF16_SUBLANES = 16
MAX_SUBLANE_STRIDE = 4
HALVES = GROUP_W // LANES
SGU_CHUNK = 128
SGU_GROUPS = 4
SGU_W = 512
SGU_GROUP_W = SGU_W // SGU_GROUPS
N_BRANCH = 2
MEM_HEADS = 4
MEM_HEAD_DIM = 128
MEM_W = MEM_HEADS * MEM_HEAD_DIM
D_FF = 2816
EPS = 1e-6
LOG2E = math.log2(math.e)

ATTN_TILE = 2048
TM = 1024
TM_CROSS = 2048
SUB_QKV = 256
SUB_MIX = 256
SUB_CROSS = 512
SUB_FFN = 256
VMEM_LIMIT = 48 * 1024 * 1024
VMEM_LIMIT_BIG = 56 * 1024 * 1024
VMEM_LIMIT_ATTN = 50 * 1024 * 1024
VMEM_LIMIT_MIX = 40 * 1024 * 1024
VMEM_LIMIT_CROSS = 40 * 1024 * 1024
VMEM_LIMIT_FFN = 28 * 1024 * 1024

F32 = jnp.float32
BF16 = jnp.bfloat16


def _rms(xf, g):
    r = lax.rsqrt(jnp.mean(xf * xf, axis=-1, keepdims=True) + EPS)
    return xf * r * g


def _alibi_slopes_grouped():
    def pow2(n):
        start = 2.0 ** (-8.0 / n)
        return [start ** (i + 1) for i in range(n)]
    n = N_ATTN_HEADS
    if math.log2(n).is_integer():
        s = pow2(n)
    else:
        c = 2 ** int(math.floor(math.log2(n)))
        s = pow2(c) + pow2(2 * c)[0::2][: n - c]
    s = np.array(sorted(s, reverse=True), dtype=np.float32)
    return s.reshape(N_GROUPS, HEADS_PER_GROUP)


def _attn_bias(gi):
    window, dil = DIL_GROUPS[gi]
    n_back = window // dil
    steps = (np.arange(BLK)[:, None] + BLK) - np.arange(2 * BLK)[None, :]
    band = (steps >= 0) & (steps <= n_back)
    dist = (np.clip(steps, 0, None) * dil).astype(np.float32)
    slopes = _alibi_slopes_grouped()[gi]
    bias = -slopes[:, None, None] * dist[None]
    bias = np.where(band[None], bias.astype(np.float64) * LOG2E, -np.inf).astype(np.float32)
    return bias.reshape(HEADS_PER_GROUP * BLK, 2 * BLK)


def _const_spec(shape):
    nd = len(shape)
    return pl.BlockSpec(shape, lambda *_: (0,) * nd, pipeline_mode=pl.Buffered(1))


def _cast_specs(weights, steps):
    in_specs, out_specs, out_shapes = [], [], []
    for w, col0 in weights:
        rows, cols = w.shape
        assert rows % (steps * BF16_SUBLANES) == 0 and col0 % LANES == 0
        in_specs.append(pl.BlockSpec((rows // steps, cols), lambda i, *_: (i, 0)))
        out_specs.append(pl.BlockSpec((rows // steps, cols - col0), lambda i, *_: (i, 0)))
        out_shapes.append(jax.ShapeDtypeStruct((rows, cols - col0), BF16))
    return in_specs, out_specs, out_shapes


def _cast_slabs(in_refs, out_refs):
    for src, dst in zip(in_refs, out_refs):
        col0 = src.shape[1] - dst.shape[1]
        dst[...] = src[:, col0:].astype(BF16)


def _residue_rows(src, tmp, dil):
    n = src.shape[0]
    if dil <= MAX_SUBLANE_STRIDE:
        for r in range(dil):
            yield r, (src[pl.ds(r, n // dil, stride=dil), :] if dil > 1 else src[...])
        return
    outer = dil // MAX_SUBLANE_STRIDE
    assert outer <= MAX_SUBLANE_STRIDE
    for r0 in range(MAX_SUBLANE_STRIDE):
        tmp[r0] = src[pl.ds(r0, n // MAX_SUBLANE_STRIDE, stride=MAX_SUBLANE_STRIDE), :]
        for r1 in range(outer):
            yield r1 * MAX_SUBLANE_STRIDE + r0, tmp[r0, pl.ds(r1, n // dil, stride=outer), :]


def _qkv_kernel(x_ref, g_ref, wf_ref, *refs, n_cast):
    cast_in, refs = refs[:n_cast], refs[n_cast:]
    out_refs, cast_out = refs[:3 * N_GROUPS], refs[3 * N_GROUPS:3 * N_GROUPS + n_cast]
    w_ref, scr, scr2 = refs[-3:]

    @pl.when(pl.program_id(0) == 0)
    def _():
        w_ref[...] = wf_ref[...].astype(BF16)

    _cast_slabs(cast_in, cast_out)
    for part in range(TM // SUB_QKV):
        rows = slice(part * SUB_QKV, (part + 1) * SUB_QKV)
        a = _rms(x_ref[rows, :], g_ref[...]).astype(BF16)
        res = jnp.dot(a, w_ref[...], preferred_element_type=F32)
        for which in range(3):
            for gi, (_, dil) in enumerate(DIL_GROUPS):
                slab = (which * ATTN_W + gi * GROUP_W) // LANES
                out = out_refs[gi * 3 + which]
                out_rows = slice(part * SUB_QKV // dil, (part + 1) * SUB_QKV // dil)
                for half in range(HALVES):
                    cols = slice((slab + half) * LANES, (slab + half + 1) * LANES)
                    if dil == 1:
                        pieces = [(0, res[:, cols])]
                    else:
                        src = scr.at[part, slab + half]
                        src[...] = res[:, cols]
                        pieces = _residue_rows(src, scr2.at[part, which * HALVES + half], dil)
                    for r, val in pieces:
                        if which == 0:
                            val = val * (HEAD_DIM ** -0.5 * LOG2E)
                        lo = r * GROUP_W + half * LANES
                        out[out_rows, lo:lo + LANES] = val.astype(BF16)


def _qkv_proj(x2, g_mix, w_in, cast_weights):
    T = x2.shape[0]
    steps = T // TM
    out_shapes, out_specs = [], []
    for _, dil in DIL_GROUPS:
        for _ in range(3):
            out_shapes.append(jax.ShapeDtypeStruct((T // dil, dil * GROUP_W), BF16))
            out_specs.append(pl.BlockSpec((TM // dil, dil * GROUP_W), lambda i: (i, 0)))
    cast_in, cast_out, cast_shapes = _cast_specs(cast_weights, steps)
    return pl.pallas_call(
        functools.partial(_qkv_kernel, n_cast=len(cast_weights)),
        grid=(steps,),
        in_specs=[pl.BlockSpec((TM, D_MODEL), lambda i: (i, 0)),
                  _const_spec((1, D_MODEL)),
                  _const_spec((D_MODEL, 3 * ATTN_W))] + cast_in,
        out_specs=out_specs + cast_out,
        out_shape=out_shapes + cast_shapes,
        scratch_shapes=[pltpu.VMEM((D_MODEL, 3 * ATTN_W), BF16),
                        pltpu.VMEM((TM // SUB_QKV, 3 * ATTN_W // LANES, SUB_QKV, LANES), F32),
                        pltpu.VMEM((TM // SUB_QKV, 3 * HALVES, MAX_SUBLANE_STRIDE,
                                    SUB_QKV // MAX_SUBLANE_STRIDE, LANES), F32)],
        compiler_params=pltpu.CompilerParams(
            dimension_semantics=("arbitrary",), vmem_limit_bytes=VMEM_LIMIT_BIG),
        name="qkv_proj",
    )(x2, g_mix, w_in, *[w for w, _ in cast_weights])


def _attn_block(q, k2, v2, bias, no_prev, head_masks, low_lanes):
    zero = jnp.zeros_like(q)
    qs = jnp.concatenate([jnp.where(m, q, zero) for m in head_masks], axis=0)
    s = lax.dot_general(qs, k2, (((1,), (1,)), ((), ())),
                        preferred_element_type=F32) + bias
    if no_prev is not None:
        s = jnp.where(no_prev, -jnp.inf, s)
    mx = jnp.max(s, axis=-1, keepdims=True)
    e = jnp.exp2(s - mx)
    den = jnp.sum(e, axis=-1, keepdims=True)
    pv = jnp.dot(e.astype(BF16), v2, preferred_element_type=F32)

    def head_rows(x, h):
        return x[h * BLK:(h + 1) * BLK]

    acc, m_rep, l_rep = [], [], []
    for half in range(HALVES):
        ha, hb = 2 * half, 2 * half + 1
        cols = slice(half * LANES, (half + 1) * LANES)
        acc.append(jnp.where(low_lanes, head_rows(pv, ha)[:, cols], head_rows(pv, hb)[:, cols]))
        m_rep.append(jnp.where(low_lanes, head_rows(mx, ha), head_rows(mx, hb)))
        l_rep.append(jnp.where(low_lanes, head_rows(den, ha), head_rows(den, hb)))
    return acc, m_rep, l_rep


def _attn_kernel(*refs):
    in_refs, (y_ref, nat, stage) = refs[:6 * N_GROUPS], refs[6 * N_GROUPS:]
    first_tile = pl.program_id(1) == 0
    lane = lax.broadcasted_iota(jnp.int32, (1, GROUP_W), 1)
    head_masks = [(lane >= HEAD_DIM * h) & (lane < HEAD_DIM * (h + 1))
                  for h in range(HEADS_PER_GROUP)]
    low_lanes = lax.broadcasted_iota(jnp.int32, (1, LANES), 1) < HEAD_DIM
    key_col = lax.broadcasted_iota(jnp.int32, (1, 2 * BLK), 1)
    no_prev_first = jnp.logical_and(first_tile, key_col < BLK)

    def group_blocks(gi):
        q_ref, k_ref, kp_ref, v_ref, vp_ref, bias_ref = in_refs[6 * gi:6 * gi + 6]
        dil = DIL_GROUPS[gi][1]
        for r in range(dil):
            lanes = slice(r * GROUP_W, (r + 1) * GROUP_W)
            for i in range(ATTN_TILE // dil // BLK):
                cur = slice(i * BLK, (i + 1) * BLK)
                if i == 0:
                    kp, vp = kp_ref[:, lanes], vp_ref[:, lanes]
                else:
                    prev = slice((i - 1) * BLK, i * BLK)
                    kp, vp = k_ref[prev, lanes], v_ref[prev, lanes]
                k2 = jnp.concatenate([kp, k_ref[cur, lanes]], axis=0)
                v2 = jnp.concatenate([vp, v_ref[cur, lanes]], axis=0)
                yield r, i, _attn_block(q_ref[cur, lanes], k2, v2, bias_ref[...],
                                        no_prev_first if i == 0 else None,
                                        head_masks, low_lanes)

    for gi in range(1, N_GROUPS):
        dil = DIL_GROUPS[gi][1]
        if dil <= MAX_SUBLANE_STRIDE:
            for r, i, stats in group_blocks(gi):
                dst = pl.ds(i * BLK * dil + r, BLK, stride=dil)
                for kind, parts in enumerate(stats):
                    for half in range(HALVES):
                        nat[gi - 1, kind, half, dst, :] = parts[half]
            continue
        outer = dil // MAX_SUBLANE_STRIDE
        for r, i, stats in group_blocks(gi):
            r1, r0 = divmod(r, MAX_SUBLANE_STRIDE)
            dst = pl.ds(i * BLK * outer + r1, BLK, stride=outer)
            for kind, parts in enumerate(stats):
                for half in range(HALVES):
                    stage[kind, half, r0, dst, :] = parts[half]
        for kind in range(3):
            for half in range(HALVES):
                for r0 in range(MAX_SUBLANE_STRIDE):
                    dst = pl.ds(r0, ATTN_TILE // MAX_SUBLANE_STRIDE, stride=MAX_SUBLANE_STRIDE)
                    nat[gi - 1, kind, half, dst, :] = stage[kind, half, r0]

    for r, i, (acc0, m0, l0) in group_blocks(0):
        rows = slice(i * BLK, (i + 1) * BLK)
        for half in range(HALVES):
            accs = [acc0[half]] + [nat[g, 0, half, rows, :] for g in range(N_GROUPS - 1)]
            ms = [m0[half]] + [nat[g, 1, half, rows, :] for g in range(N_GROUPS - 1)]
            ls = [l0[half]] + [nat[g, 2, half, rows, :] for g in range(N_GROUPS - 1)]
            m_all = jnp.maximum(jnp.maximum(ms[0], ms[1]), ms[2])
            ws = [jnp.exp2(m - m_all) for m in ms]
            num = ws[0] * accs[0] + ws[1] * accs[1] + ws[2] * accs[2]
            den = ws[0] * ls[0] + ws[1] * ls[1] + ws[2] * ls[2]
            y_ref[rows, half * LANES:(half + 1) * LANES] = (num * (1.0 / den)).astype(BF16)


def _dilated_attention(qkv, batch):
    T = qkv[0].shape[0]
    tiles_per_batch = T // ATTN_TILE // batch

    def cur_map(b, t):
        return (b * tiles_per_batch + t, 0)

    operands, in_specs = [], []
    for gi, (_, dil) in enumerate(DIL_GROUPS):
        qd, kd, vd = qkv[3 * gi:3 * gi + 3]
        rows, width = ATTN_TILE // dil, dil * GROUP_W
        prev_blocks = rows // BLK

        def prev_map(b, t, prev_blocks=prev_blocks):
            return (jnp.maximum((b * tiles_per_batch + t) * prev_blocks - 1, 0), 0)

        bias = jnp.asarray(_attn_bias(gi))
        operands += [qd, kd, kd, vd, vd, bias]
        in_specs += [pl.BlockSpec((rows, width), cur_map),
                     pl.BlockSpec((rows, width), cur_map),
                     pl.BlockSpec((BLK, width), prev_map),
                     pl.BlockSpec((rows, width), cur_map),
                     pl.BlockSpec((BLK, width), prev_map),
                     _const_spec(bias.shape)]
    return pl.pallas_call(
        _attn_kernel,
        grid=(batch, tiles_per_batch),
        in_specs=in_specs,
        out_specs=pl.BlockSpec((ATTN_TILE, GROUP_W), cur_map),
        out_shape=jax.ShapeDtypeStruct((T, GROUP_W), BF16),
        scratch_shapes=[pltpu.VMEM((N_GROUPS - 1, 3, HALVES, ATTN_TILE, LANES), F32),
                        pltpu.VMEM((3, HALVES, MAX_SUBLANE_STRIDE,
                                    ATTN_TILE // MAX_SUBLANE_STRIDE, LANES), F32)],
        compiler_params=pltpu.CompilerParams(
            dimension_semantics=("arbitrary", "arbitrary"), vmem_limit_bytes=VMEM_LIMIT_ATTN),
        name="dilated_attn",
    )(*operands)


def _mixer_kernel(x_ref, y_ref, g_ref, w_ref, bg_ref, ws_ref, bs_ref, gs_ref,
                  wba_ref, wbs_ref, wo_ref, *refs, n_cast):
    cast_in, h_ref, cast_out = refs[:n_cast], refs[n_cast], refs[n_cast + 1:]
    _cast_slabs(cast_in, cast_out)
    row = lax.broadcasted_iota(jnp.int32, (SGU_CHUNK, SGU_CHUNK), 0)
    col = lax.broadcasted_iota(jnp.int32, (SGU_CHUNK, SGU_CHUNK), 1)
    ws = [jnp.where(row >= col, ws_ref[g], 0.0).astype(BF16) for g in range(SGU_GROUPS)]
    bias = jnp.concatenate(
        [jnp.broadcast_to(jnp.sum(jnp.where(row == col, bs_ref[g:g + 1, :], 0.0),
                                  axis=1, keepdims=True), (SGU_CHUNK, SGU_GROUP_W))
         for g in range(SGU_GROUPS)], axis=1)
    for part in range(TM // SUB_MIX):
        rows = slice(part * SUB_MIX, (part + 1) * SUB_MIX)
        x = x_ref[rows, :]
        a = _rms(x, g_ref[...]).astype(BF16)
        proj = jnp.dot(a, w_ref[...], preferred_element_type=F32)

        z = jax.nn.gelu(proj[:, :2 * SGU_W], approximate=True)
        u = z[:, :SGU_W]
        v = _rms(z[:, SGU_W:], gs_ref[...]).astype(BF16)
        n_chunks = SUB_MIX // SGU_CHUNK
        mixed = []
        for g in range(SGU_GROUPS):
            gl = slice(g * SGU_GROUP_W, (g + 1) * SGU_GROUP_W)
            vg = jnp.concatenate([v[c * SGU_CHUNK:(c + 1) * SGU_CHUNK, gl]
                                  for c in range(n_chunks)], axis=1)
            mixed.append(jnp.dot(ws[g], vg, preferred_element_type=F32))
        chunks = [jnp.concatenate([m[:, c * SGU_GROUP_W:(c + 1) * SGU_GROUP_W] for m in mixed],
                                  axis=1) + bias for c in range(n_chunks)]
        y_sgu = u * jnp.concatenate(chunks, axis=0)

        gates = 1.0 / (1.0 + jnp.exp(-(proj[:, 2 * SGU_W:] + bg_ref[...])))
        merged = (gates[:, :D_MODEL] * jnp.dot(y_ref[rows, :], wba_ref[...],
                                                preferred_element_type=F32)
                  + gates[:, D_MODEL:] * jnp.dot(y_sgu.astype(BF16), wbs_ref[...],
                                                 preferred_element_type=F32))
        h_ref[rows, :] = x + jnp.dot(merged.astype(BF16), wo_ref[...],
                                     preferred_element_type=F32)


def _mixer(x2, y_attn, g_mix, w_uvgl, b_gate, w_s, b_s, g_sgu, w_ba, w_bs, w_out, cast_weights):
    T = x2.shape[0]
    steps = T // TM
    row_spec = lambda w: pl.BlockSpec((TM, w), lambda i: (i, 0))
    consts = (g_mix, w_uvgl, b_gate, w_s, b_s, g_sgu, w_ba, w_bs, w_out)
    const_specs = [pl.BlockSpec(memory_space=pltpu.VMEM) if c.dtype == BF16
                   else _const_spec(c.shape) for c in consts]
    cast_in, cast_out, cast_shapes = _cast_specs(cast_weights, steps)
    return pl.pallas_call(
        functools.partial(_mixer_kernel, n_cast=len(cast_weights)),
        grid=(steps,),
        in_specs=[row_spec(D_MODEL), row_spec(GROUP_W)] + const_specs + cast_in,
        out_specs=[row_spec(D_MODEL)] + cast_out,
        out_shape=[jax.ShapeDtypeStruct((T, D_MODEL), F32)] + cast_shapes,
        compiler_params=pltpu.CompilerParams(
            dimension_semantics=("arbitrary",), vmem_limit_bytes=VMEM_LIMIT_MIX),
        name="mixer_tail",
    )(x2, y_attn, *consts, *[w for w, _ in cast_weights])


def _cross_kernel(h_ref, mem_ref, gm_ref, wkv_ref, g_ref, wq_ref, wo_ref, out_ref, kv_ref):
    @pl.when(pl.program_id(1) == 0)
    def _():
        m = _rms(mem_ref[...], gm_ref[...]).astype(BF16)
        kv_ref[...] = jnp.dot(m, wkv_ref[...], preferred_element_type=F32).astype(BF16)

    for part in range(TM_CROSS // SUB_CROSS):
        rows = slice(part * SUB_CROSS, (part + 1) * SUB_CROSS)
        h = h_ref[rows, :]
        c = _rms(h, g_ref[...]).astype(BF16)
        q = jnp.dot(c, wq_ref[...], preferred_element_type=F32).astype(BF16)
        heads = []
        for hd in range(MEM_HEADS):
            cols = slice(hd * MEM_HEAD_DIM, (hd + 1) * MEM_HEAD_DIM)
            k = kv_ref[:, cols]
            v = kv_ref[:, MEM_W + hd * MEM_HEAD_DIM:MEM_W + (hd + 1) * MEM_HEAD_DIM]
            s = lax.dot_general(q[:, cols], k, (((1,), (1,)), ((), ())),
                                preferred_element_type=F32) * (MEM_HEAD_DIM ** -0.5)
            e = jnp.exp(s - jnp.max(s, axis=-1, keepdims=True))
            den = jnp.sum(e, axis=-1, keepdims=True)
            o = jnp.dot(e.astype(BF16), v, preferred_element_type=F32) * (1.0 / den)
            heads.append(o.astype(BF16))
        o = jnp.concatenate(heads, axis=1)
        out_ref[rows, :] = h + jnp.dot(o, wo_ref[...], preferred_element_type=F32)


def _cross_attention(h, mem2, g_mem, w_kv, g_cross, w_q, w_o, batch):
    T = h.shape[0]
    blocks_per_batch = T // batch // TM_CROSS
    mem_len = mem2.shape[0] // batch
    consts = (g_mem, w_kv, g_cross, w_q, w_o)
    row_map = lambda b, i: (b * blocks_per_batch + i, 0)
    return pl.pallas_call(
        _cross_kernel,
        grid=(batch, blocks_per_batch),
        in_specs=[pl.BlockSpec((TM_CROSS, D_MODEL), row_map),
                  pl.BlockSpec((mem_len, D_MODEL), lambda b, i: (b, 0))]
                 + [_const_spec(c.shape) for c in consts],
        out_specs=pl.BlockSpec((TM_CROSS, D_MODEL), row_map),
        out_shape=jax.ShapeDtypeStruct((T, D_MODEL), F32),
        scratch_shapes=[pltpu.VMEM((mem_len, 2 * MEM_W), BF16)],
        compiler_params=pltpu.CompilerParams(
            dimension_semantics=("arbitrary", "arbitrary"), vmem_limit_bytes=VMEM_LIMIT_CROSS),
        name="cross_attn",
    )(h, mem2, *consts)


def _ffn_kernel(h_ref, g_ref, wgu_ref, wd_ref, gf_ref, out_ref):
    for part in range(TM // SUB_FFN):
        rows = slice(part * SUB_FFN, (part + 1) * SUB_FFN)
        h = h_ref[rows, :]
        f = _rms(h, g_ref[...]).astype(BF16)
        gu = jnp.dot(f, wgu_ref[...], preferred_element_type=F32)
        gt, up = gu[:, :D_FF], gu[:, D_FF:]
        act = (gt * (1.0 / (1.0 + jnp.exp(-gt))) * up).astype(BF16)
        acc = h + jnp.dot(act, wd_ref[...], preferred_element_type=F32)
        out_ref[rows, :] = _rms(acc, gf_ref[...])


def _ffn(h, g_ffn, w_gu, w_down, g_final):
    T = h.shape[0]
    consts = (g_ffn, w_gu, w_down, g_final)
    return pl.pallas_call(
        _ffn_kernel,
        grid=(T // TM,),
        in_specs=[pl.BlockSpec((TM, D_MODEL), lambda i: (i, 0)),
                  _const_spec(g_ffn.shape),
                  pl.BlockSpec(memory_space=pltpu.VMEM),
                  pl.BlockSpec(memory_space=pltpu.VMEM),
                  _const_spec(g_final.shape)],
        out_specs=pl.BlockSpec((TM, D_MODEL), lambda i: (i, 0)),
        out_shape=jax.ShapeDtypeStruct((T, D_MODEL), F32),
        compiler_params=pltpu.CompilerParams(
            dimension_semantics=("arbitrary",), vmem_limit_bytes=VMEM_LIMIT_FFN),
        name="ffn",
    )(h, *consts)


def kernel(x, mem, g_mix, w_in, b_gate, w_sgu_spatial, b_sgu_spatial, g_sgu, w_branch_attn,
           w_branch_sgu, w_out, g_cross, g_mem, w_q_cross, w_kv_cross, w_o_cross, g_ffn,
           w_gate_up, w_down, g_final):
    B, S, D = x.shape
    assert D == D_MODEL and S % ATTN_TILE == 0 and w_in.shape[0] == 1
    T = B * S
    x2 = x.reshape(T, D)
    row = lambda v: v.reshape(1, -1)

    g_mix2 = row(g_mix[0])
    w_in2 = w_in.reshape(D, w_in.shape[-1])

    *qkv, w_uvgl, w_ba, w_bs, w_o1, w_q, w_kv, w_o2 = _qkv_proj(
        x2, g_mix2, w_in2,
        ((w_in2, 3 * ATTN_W), (w_branch_attn[0], 0), (w_branch_sgu[0], 0), (w_out[0], 0),
         (w_q_cross[0], 0), (w_kv_cross[0], 0), (w_o_cross[0], 0)))
    y_attn = _dilated_attention(qkv, batch=B)

    h, w_gu, w_dn = _mixer(
        x2, y_attn, g_mix2, w_uvgl, row(b_gate[0]), w_sgu_spatial[0], b_sgu_spatial[0],
        row(g_sgu[0]), w_ba, w_bs, w_o1, ((w_gate_up[0], 0), (w_down[0], 0)))

    h = _cross_attention(h, mem.reshape(B * mem.shape[1], D), row(g_mem[0]), w_kv,
                         row(g_cross[0]), w_q, w_o2, batch=B)
    out = _ffn(h, row(g_ffn[0]), w_gu, w_dn, row(g_final))
    return out.reshape(B, S, D)
```

```python
import functools
import math

import jax
import jax.numpy as jnp
import numpy as np
from jax import lax
from jax.experimental import pallas as pl
from jax.experimental.pallas import tpu as pltpu

D_MODEL = 1024
HEAD_DIM = 64
DIL_GROUPS = ((128, 1), (512, 4), (2048, 16))
N_GROUPS = 3
HEADS_PER_GROUP = 4
N_ATTN_HEADS = N_GROUPS * HEADS_PER_GROUP
ATTN_W = N_ATTN_HEADS * HEAD_DIM
GROUP_W = HEADS_PER_GROUP * HEAD_DIM
BLK = 128
LANES = 128
BF16_SUBLANES = 16
MAX_SUBLANE_STRIDE = 4
HALVES = GROUP_W // LANES
SGU_CHUNK = 128
SGU_GROUPS = 4
SGU_W = 512
SGU_GROUP_W = SGU_W // SGU_GROUPS
N_BRANCH = 2
MEM_HEADS = 4
MEM_HEAD_DIM = 128
MEM_W = MEM_HEADS * MEM_HEAD_DIM
D_FF = 2816
FF_TILE = 256
EPS = 1e-6
LOG2E = math.log2(math.e)

ATTN_TILE = 2048
TM = 1024
TM_CROSS = 2048
SUB_QKV = 512
SUB_MIX = 512
SUB_CROSS = 512
SUB_FFN = 256
VMEM_LIMIT = 48 * 1024 * 1024
VMEM_LIMIT_BIG = 56 * 1024 * 1024
VMEM_LIMIT_ATTN = 50 * 1024 * 1024
VMEM_LIMIT_MIX = 40 * 1024 * 1024
VMEM_LIMIT_CROSS = 40 * 1024 * 1024
VMEM_LIMIT_FFN = 28 * 1024 * 1024

F32 = jnp.float32
BF16 = jnp.bfloat16


def _rms(xf, g):
    r = lax.rsqrt(jnp.mean(xf * xf, axis=-1, keepdims=True) + EPS)
    return xf * r * g


def _alibi_slopes_grouped():
    def pow2(n):
        start = 2.0 ** (-8.0 / n)
        return [start ** (i + 1) for i in range(n)]
    n = N_ATTN_HEADS
    if math.log2(n).is_integer():
        s = pow2(n)
    else:
        c = 2 ** int(math.floor(math.log2(n)))
        s = pow2(c) + pow2(2 * c)[0::2][: n - c]
    s = np.array(sorted(s, reverse=True), dtype=np.float32)
    return s.reshape(N_GROUPS, HEADS_PER_GROUP)


def _attn_bias(gi):
    window, dil = DIL_GROUPS[gi]
    n_back = window // dil
    steps = (np.arange(BLK)[:, None] + BLK) - np.arange(2 * BLK)[None, :]
    band = (steps >= 0) & (steps <= n_back)
    dist = (np.clip(steps, 0, None) * dil).astype(np.float32)
    slopes = _alibi_slopes_grouped()[gi]
    bias = -slopes[:, None, None] * dist[None]
    bias = np.where(band[None], bias.astype(np.float64) * LOG2E, -np.inf).astype(np.float32)
    return bias.reshape(HEADS_PER_GROUP * BLK, 2 * BLK)


def _const_spec(shape):
    nd = len(shape)
    return pl.BlockSpec(shape, lambda *_: (0,) * nd, pipeline_mode=pl.Buffered(1))


def _cast_specs(weights, steps):
    in_specs, out_specs, out_shapes = [], [], []
    for w, col0 in weights:
        rows, cols = w.shape
        assert rows % (steps * BF16_SUBLANES) == 0 and col0 % LANES == 0
        in_specs.append(pl.BlockSpec((rows // steps, cols), lambda i, *_: (i, 0)))
        out_specs.append(pl.BlockSpec((rows // steps, cols - col0), lambda i, *_: (i, 0)))
        out_shapes.append(jax.ShapeDtypeStruct((rows, cols - col0), BF16))
    return in_specs, out_specs, out_shapes


def _cast_slabs(in_refs, out_refs):
    for src, dst in zip(in_refs, out_refs):
        col0 = src.shape[1] - dst.shape[1]
        dst[...] = src[:, col0:].astype(BF16)


def _residue_rows(src, tmp, dil):
    n = src.shape[0]
    if dil <= MAX_SUBLANE_STRIDE:
        for r in range(dil):
            yield r, (src[pl.ds(r, n // dil, stride=dil), :] if dil > 1 else src[...])
        return
    outer = dil // MAX_SUBLANE_STRIDE
    assert outer <= MAX_SUBLANE_STRIDE
    for r0 in range(MAX_SUBLANE_STRIDE):
        tmp[r0] = src[pl.ds(r0, n // MAX_SUBLANE_STRIDE, stride=MAX_SUBLANE_STRIDE), :]
        for r1 in range(outer):
            yield r1 * MAX_SUBLANE_STRIDE + r0, tmp[r0, pl.ds(r1, n // dil, stride=outer), :]


def _qkv_kernel(x_ref, g_ref, wf_ref, *refs, n_cast):
    cast_in, refs = refs[:n_cast], refs[n_cast:]
    out_refs, cast_out = refs[:3 * N_GROUPS], refs[3 * N_GROUPS:3 * N_GROUPS + n_cast]
    w_ref, scr, scr2 = refs[-3:]

    @pl.when(pl.program_id(0) == 0)
    def _():
        w_ref[...] = wf_ref[...].astype(BF16)

    _cast_slabs(cast_in, cast_out)
    for part in range(TM // SUB_QKV):
        rows = slice(part * SUB_QKV, (part + 1) * SUB_QKV)
        a = _rms(x_ref[rows, :], g_ref[...]).astype(BF16)
        res = jnp.dot(a, w_ref[...], preferred_element_type=F32)
        for which in range(3):
            for gi, (_, dil) in enumerate(DIL_GROUPS):
                slab = (which * ATTN_W + gi * GROUP_W) // LANES
                out = out_refs[gi * 3 + which]
                out_rows = slice(part * SUB_QKV // dil, (part + 1) * SUB_QKV // dil)
                for half in range(HALVES):
                    cols = slice((slab + half) * LANES, (slab + half + 1) * LANES)
                    if dil == 1:
                        pieces = [(0, res[:, cols])]
                    else:
                        src = scr.at[part, slab + half]
                        src[...] = res[:, cols]
                        pieces = _residue_rows(src, scr2.at[part, which * HALVES + half], dil)
                    for r, val in pieces:
                        if which == 0:
                            val = val * (HEAD_DIM ** -0.5 * LOG2E)
                        lo = r * GROUP_W + half * LANES
                        out[out_rows, lo:lo + LANES] = val.astype(BF16)


def _qkv_proj(x2, g_mix, w_in, cast_weights):
    T = x2.shape[0]
    steps = T // TM
    out_shapes, out_specs = [], []
    for _, dil in DIL_GROUPS:
        for _ in range(3):
            out_shapes.append(jax.ShapeDtypeStruct((T // dil, dil * GROUP_W), BF16))
            out_specs.append(pl.BlockSpec((TM // dil, dil * GROUP_W), lambda i: (i, 0)))
    cast_in, cast_out, cast_shapes = _cast_specs(cast_weights, steps)
    return pl.pallas_call(
        functools.partial(_qkv_kernel, n_cast=len(cast_weights)),
        grid=(steps,),
        in_specs=[pl.BlockSpec((TM, D_MODEL), lambda i: (i, 0)),
                  _const_spec((1, D_MODEL)),
                  _const_spec((D_MODEL, 3 * ATTN_W))] + cast_in,
        out_specs=out_specs + cast_out,
        out_shape=out_shapes + cast_shapes,
        scratch_shapes=[pltpu.VMEM((D_MODEL, 3 * ATTN_W), BF16),
                        pltpu.VMEM((TM // SUB_QKV, 3 * ATTN_W // LANES, SUB_QKV, LANES), F32),
                        pltpu.VMEM((TM // SUB_QKV, 3 * HALVES, MAX_SUBLANE_STRIDE,
                                    SUB_QKV // MAX_SUBLANE_STRIDE, LANES), F32)],
        compiler_params=pltpu.CompilerParams(
            dimension_semantics=("arbitrary",), vmem_limit_bytes=VMEM_LIMIT_BIG),
        name="qkv_proj",
    )(x2, g_mix, w_in, *[w for w, _ in cast_weights])


def _attn_block(q, k2, v2, bias, no_prev, head_masks, low_lanes):
    zero = jnp.zeros_like(q)
    qs = jnp.concatenate([jnp.where(m, q, zero) for m in head_masks], axis=0)
    s = lax.dot_general(qs, k2, (((1,), (1,)), ((), ())),
                        preferred_element_type=F32) + bias
    if no_prev is not None:
        s = jnp.where(no_prev, -jnp.inf, s)
    mx = jnp.max(s, axis=-1, keepdims=True)
    e = jnp.exp2(s - mx)
    den = jnp.sum(e, axis=-1, keepdims=True)
    pv = jnp.dot(e.astype(BF16), v2, preferred_element_type=F32)

    def head_rows(x, h):
        return x[h * BLK:(h + 1) * BLK]

    acc, m_rep, l_rep = [], [], []
    for half in range(HALVES):
        ha, hb = 2 * half, 2 * half + 1
        cols = slice(half * LANES, (half + 1) * LANES)
        acc.append(jnp.where(low_lanes, head_rows(pv, ha)[:, cols], head_rows(pv, hb)[:, cols]))
        m_rep.append(jnp.where(low_lanes, head_rows(mx, ha), head_rows(mx, hb)))
        l_rep.append(jnp.where(low_lanes, head_rows(den, ha), head_rows(den, hb)))
    return acc, m_rep, l_rep


def _attn_kernel(*refs):
    in_refs, (y_ref, nat, stage) = refs[:6 * N_GROUPS], refs[6 * N_GROUPS:]
    first_tile = pl.program_id(1) == 0
    lane = lax.broadcasted_iota(jnp.int32, (1, GROUP_W), 1)
    head_masks = [(lane >= HEAD_DIM * h) & (lane < HEAD_DIM * (h + 1))
                  for h in range(HEADS_PER_GROUP)]
    low_lanes = lax.broadcasted_iota(jnp.int32, (1, LANES), 1) < HEAD_DIM
    key_col = lax.broadcasted_iota(jnp.int32, (1, 2 * BLK), 1)
    no_prev_first = jnp.logical_and(first_tile, key_col < BLK)

    def group_blocks(gi):
        q_ref, k_ref, kp_ref, v_ref, vp_ref, bias_ref = in_refs[6 * gi:6 * gi + 6]
        dil = DIL_GROUPS[gi][1]
        for r in range(dil):
            lanes = slice(r * GROUP_W, (r + 1) * GROUP_W)
            for i in range(ATTN_TILE // dil // BLK):
                cur = slice(i * BLK, (i + 1) * BLK)
                if i == 0:
                    kp, vp = kp_ref[:, lanes], vp_ref[:, lanes]
                else:
                    prev = slice((i - 1) * BLK, i * BLK)
                    kp, vp = k_ref[prev, lanes], v_ref[prev, lanes]
                k2 = jnp.concatenate([kp, k_ref[cur, lanes]], axis=0)
                v2 = jnp.concatenate([vp, v_ref[cur, lanes]], axis=0)
                yield r, i, _attn_block(q_ref[cur, lanes], k2, v2, bias_ref[...],
                                        no_prev_first if i == 0 else None,
                                        head_masks, low_lanes)

    for gi in range(1, N_GROUPS):
        dil = DIL_GROUPS[gi][1]
        if dil <= MAX_SUBLANE_STRIDE:
            for r, i, stats in group_blocks(gi):
                dst = pl.ds(i * BLK * dil + r, BLK, stride=dil)
                for kind, parts in enumerate(stats):
                    for half in range(HALVES):
                        nat[gi - 1, kind, half, dst, :] = parts[half]
            continue
        outer = dil // MAX_SUBLANE_STRIDE
        for r, i, stats in group_blocks(gi):
            r1, r0 = divmod(r, MAX_SUBLANE_STRIDE)
            dst = pl.ds(i * BLK * outer + r1, BLK, stride=outer)
            for kind, parts in enumerate(stats):
                for half in range(HALVES):
                    stage[kind, half, r0, dst, :] = parts[half]
        for kind in range(3):
            for half in range(HALVES):
                for r0 in range(MAX_SUBLANE_STRIDE):
                    dst = pl.ds(r0, ATTN_TILE // MAX_SUBLANE_STRIDE, stride=MAX_SUBLANE_STRIDE)
                    nat[gi - 1, kind, half, dst, :] = stage[kind, half, r0]

    for r, i, (acc0, m0, l0) in group_blocks(0):
        rows = slice(i * BLK, (i + 1) * BLK)
        for half in range(HALVES):
            accs = [acc0[half]] + [nat[g, 0, half, rows, :] for g in range(N_GROUPS - 1)]
            ms = [m0[half]] + [nat[g, 1, half, rows, :] for g in range(N_GROUPS - 1)]
            ls = [l0[half]] + [nat[g, 2, half, rows, :] for g in range(N_GROUPS - 1)]
            m_all = jnp.maximum(jnp.maximum(ms[0], ms[1]), ms[2])
            ws = [jnp.exp2(m - m_all) for m in ms]
            num = ws[0] * accs[0] + ws[1] * accs[1] + ws[2] * accs[2]
            den = ws[0] * ls[0] + ws[1] * ls[1] + ws[2] * ls[2]
            y_ref[rows, half * LANES:(half + 1) * LANES] = (num * (1.0 / den)).astype(BF16)


def _dilated_attention(qkv, batch):
    T = qkv[0].shape[0]
    tiles_per_batch = T // ATTN_TILE // batch

    def cur_map(b, t):
        return (b * tiles_per_batch + t, 0)

    operands, in_specs = [], []
    for gi, (_, dil) in enumerate(DIL_GROUPS):
        qd, kd, vd = qkv[3 * gi:3 * gi + 3]
        rows, width = ATTN_TILE // dil, dil * GROUP_W
        prev_blocks = rows // BLK

        def prev_map(b, t, prev_blocks=prev_blocks):
            return (jnp.maximum((b * tiles_per_batch + t) * prev_blocks - 1, 0), 0)

        bias = jnp.asarray(_attn_bias(gi))
        operands += [qd, kd, kd, vd, vd, bias]
        in_specs += [pl.BlockSpec((rows, width), cur_map),
                     pl.BlockSpec((rows, width), cur_map),
                     pl.BlockSpec((BLK, width), prev_map),
                     pl.BlockSpec((rows, width), cur_map),
                     pl.BlockSpec((BLK, width), prev_map),
                     _const_spec(bias.shape)]
    return pl.pallas_call(
        _attn_kernel,
        grid=(batch, tiles_per_batch),
        in_specs=in_specs,
        out_specs=pl.BlockSpec((ATTN_TILE, GROUP_W), cur_map),
        out_shape=jax.ShapeDtypeStruct((T, GROUP_W), BF16),
        scratch_shapes=[pltpu.VMEM((N_GROUPS - 1, 3, HALVES, ATTN_TILE, LANES), F32),
                        pltpu.VMEM((3, HALVES, MAX_SUBLANE_STRIDE,
                                    ATTN_TILE // MAX_SUBLANE_STRIDE, LANES), F32)],
        compiler_params=pltpu.CompilerParams(
            dimension_semantics=("arbitrary", "arbitrary"), vmem_limit_bytes=VMEM_LIMIT_ATTN),
        name="dilated_attn",
    )(*operands)


def _mixer_kernel(x_ref, y_ref, g_ref, w_ref, bg_ref, ws_ref, bs_ref, gs_ref,
                  wba_ref, wbs_ref, wo_ref, *refs, n_cast):
    cast_in, h_ref, cast_out = refs[:n_cast], refs[n_cast], refs[n_cast + 1:]
    for j in range(D_FF // FF_TILE):
        for half in range(2):
            src_cols = slice(half * D_FF + j * FF_TILE, half * D_FF + (j + 1) * FF_TILE)
            dst_cols = slice((2 * j + half) * FF_TILE, (2 * j + half + 1) * FF_TILE)
            cast_out[0][:, dst_cols] = cast_in[0][:, src_cols].astype(BF16)
    _cast_slabs(cast_in[1:], cast_out[1:])
    row = lax.broadcasted_iota(jnp.int32, (SGU_CHUNK, SGU_CHUNK), 0)
    col = lax.broadcasted_iota(jnp.int32, (SGU_CHUNK, SGU_CHUNK), 1)
    ws = [jnp.where(row >= col, ws_ref[g], 0.0).astype(BF16) for g in range(SGU_GROUPS)]
    bias = jnp.concatenate(
        [jnp.broadcast_to(jnp.sum(jnp.where(row == col, bs_ref[g:g + 1, :], 0.0),
                                  axis=1, keepdims=True), (SGU_CHUNK, SGU_GROUP_W))
         for g in range(SGU_GROUPS)], axis=1)
    for part in range(TM // SUB_MIX):
        rows = slice(part * SUB_MIX, (part + 1) * SUB_MIX)
        x = x_ref[rows, :]
        a = _rms(x, g_ref[...]).astype(BF16)
        proj = jnp.dot(a, w_ref[...], preferred_element_type=F32)

        z = jax.nn.gelu(proj[:, :2 * SGU_W], approximate=True)
        u = z[:, :SGU_W]
        v = _rms(z[:, SGU_W:], gs_ref[...]).astype(BF16)
        n_chunks = SUB_MIX // SGU_CHUNK
        mixed = []
        for g in range(SGU_GROUPS):
            gl = slice(g * SGU_GROUP_W, (g + 1) * SGU_GROUP_W)
            vg = jnp.concatenate([v[c * SGU_CHUNK:(c + 1) * SGU_CHUNK, gl]
                                  for c in range(n_chunks)], axis=1)
            mixed.append(jnp.dot(ws[g], vg, preferred_element_type=F32))
        chunks = [jnp.concatenate([m[:, c * SGU_GROUP_W:(c + 1) * SGU_GROUP_W] for m in mixed],
                                  axis=1) + bias for c in range(n_chunks)]
        y_sgu = u * jnp.concatenate(chunks, axis=0)

        gates = 1.0 / (1.0 + jnp.exp(-(proj[:, 2 * SGU_W:] + bg_ref[...])))
        merged = (gates[:, :D_MODEL] * jnp.dot(y_ref[rows, :], wba_ref[...],
                                                preferred_element_type=F32)
                  + gates[:, D_MODEL:] * jnp.dot(y_sgu.astype(BF16), wbs_ref[...],
                                                 preferred_element_type=F32))
        h_ref[rows, :] = x + jnp.dot(merged.astype(BF16), wo_ref[...],
                                     preferred_element_type=F32)


def _mixer(x2, y_attn, g_mix, w_uvgl, b_gate, w_s, b_s, g_sgu, w_ba, w_bs, w_out, cast_weights):
    T = x2.shape[0]
    steps = T // TM
    row_spec = lambda w: pl.BlockSpec((TM, w), lambda i: (i, 0))
    consts = (g_mix, w_uvgl, b_gate, w_s, b_s, g_sgu, w_ba, w_bs, w_out)
    const_specs = [pl.BlockSpec(memory_space=pltpu.VMEM) if c.dtype == BF16
                   else _const_spec(c.shape) for c in consts]
    cast_in, cast_out, cast_shapes = _cast_specs(cast_weights, steps)
    return pl.pallas_call(
        functools.partial(_mixer_kernel, n_cast=len(cast_weights)),
        grid=(steps,),
        in_specs=[row_spec(D_MODEL), row_spec(GROUP_W)] + const_specs + cast_in,
        out_specs=[row_spec(D_MODEL)] + cast_out,
        out_shape=[jax.ShapeDtypeStruct((T, D_MODEL), F32)] + cast_shapes,
        compiler_params=pltpu.CompilerParams(
            dimension_semantics=("arbitrary",), vmem_limit_bytes=VMEM_LIMIT_MIX),
        name="mixer_tail",
    )(x2, y_attn, *consts, *[w for w, _ in cast_weights])


def _cross_kernel(h_ref, mem_ref, gm_ref, wkv_ref, g_ref, wq_ref, wo_ref, out_ref, kv_ref):
    @pl.when(pl.program_id(1) == 0)
    def _():
        m = _rms(mem_ref[...], gm_ref[...]).astype(BF16)
        kv_ref[...] = jnp.dot(m, wkv_ref[...], preferred_element_type=F32).astype(BF16)

    for part in range(TM_CROSS // SUB_CROSS):
        rows = slice(part * SUB_CROSS, (part + 1) * SUB_CROSS)
        h = h_ref[rows, :]
        c = _rms(h, g_ref[...]).astype(BF16)
        q = jnp.dot(c, wq_ref[...], preferred_element_type=F32).astype(BF16)
        heads = []
        for hd in range(MEM_HEADS):
            cols = slice(hd * MEM_HEAD_DIM, (hd + 1) * MEM_HEAD_DIM)
            k = kv_ref[:, cols]
            v = kv_ref[:, MEM_W + hd * MEM_HEAD_DIM:MEM_W + (hd + 1) * MEM_HEAD_DIM]
            s = lax.dot_general(q[:, cols], k, (((1,), (1,)), ((), ())),
                                preferred_element_type=F32) * (MEM_HEAD_DIM ** -0.5)
            e = jnp.exp(s - jnp.max(s, axis=-1, keepdims=True))
            den = jnp.sum(e, axis=-1, keepdims=True)
            o = jnp.dot(e.astype(BF16), v, preferred_element_type=F32) * (1.0 / den)
            heads.append(o.astype(BF16))
        o = jnp.concatenate(heads, axis=1)
        out_ref[rows, :] = h + jnp.dot(o, wo_ref[...], preferred_element_type=F32)


def _cross_attention(h, mem2, g_mem, w_kv, g_cross, w_q, w_o, batch):
    T = h.shape[0]
    blocks_per_batch = T // batch // TM_CROSS
    mem_len = mem2.shape[0] // batch
    consts = (g_mem, w_kv, g_cross, w_q, w_o)
    row_map = lambda b, i: (b * blocks_per_batch + i, 0)
    return pl.pallas_call(
        _cross_kernel,
        grid=(batch, blocks_per_batch),
        in_specs=[pl.BlockSpec((TM_CROSS, D_MODEL), row_map),
                  pl.BlockSpec((mem_len, D_MODEL), lambda b, i: (b, 0))]
                 + [_const_spec(c.shape) for c in consts],
        out_specs=pl.BlockSpec((TM_CROSS, D_MODEL), row_map),
        out_shape=jax.ShapeDtypeStruct((T, D_MODEL), F32),
        scratch_shapes=[pltpu.VMEM((mem_len, 2 * MEM_W), BF16)],
        compiler_params=pltpu.CompilerParams(
            dimension_semantics=("arbitrary", "arbitrary"), vmem_limit_bytes=VMEM_LIMIT_CROSS),
        name="cross_attn",
    )(h, mem2, *consts)


def _ffn_kernel(h_ref, g_ref, wgu_ref, wd_ref, gf_ref, out_ref):
    for part in range(TM // SUB_FFN):
        rows = slice(part * SUB_FFN, (part + 1) * SUB_FFN)
        h = h_ref[rows, :]
        f = _rms(h, g_ref[...]).astype(BF16)
        gu = jnp.dot(f, wgu_ref[...], preferred_element_type=F32)
        tiles = []
        for j in range(D_FF // FF_TILE):
            gt = gu[:, 2 * j * FF_TILE:(2 * j + 1) * FF_TILE]
            up = gu[:, (2 * j + 1) * FF_TILE:(2 * j + 2) * FF_TILE]
            tiles.append(gt * (1.0 / (1.0 + jnp.exp(-gt))) * up)
        act = jnp.concatenate(tiles, axis=1).astype(BF16)
        acc = h + jnp.dot(act, wd_ref[...], preferred_element_type=F32)
        out_ref[rows, :] = _rms(acc, gf_ref[...])


def _ffn(h, g_ffn, w_gu, w_down, g_final):
    T = h.shape[0]
    consts = (g_ffn, w_gu, w_down, g_final)
    return pl.pallas_call(
        _ffn_kernel,
        grid=(T // TM,),
        in_specs=[pl.BlockSpec((TM, D_MODEL), lambda i: (i, 0)),
                  _const_spec(g_ffn.shape),
                  pl.BlockSpec(memory_space=pltpu.VMEM),
                  pl.BlockSpec(memory_space=pltpu.VMEM),
                  _const_spec(g_final.shape)],
        out_specs=pl.BlockSpec((TM, D_MODEL), lambda i: (i, 0)),
        out_shape=jax.ShapeDtypeStruct((T, D_MODEL), F32),
        compiler_params=pltpu.CompilerParams(
            dimension_semantics=("arbitrary",), vmem_limit_bytes=VMEM_LIMIT_FFN),
        name="ffn",
    )(h, *consts)


def kernel(x, mem, g_mix, w_in, b_gate, w_sgu_spatial, b_sgu_spatial, g_sgu, w_branch_attn,
           w_branch_sgu, w_out, g_cross, g_mem, w_q_cross, w_kv_cross, w_o_cross, g_ffn,
           w_gate_up, w_down, g_final):
    B, S, D = x.shape
    assert D == D_MODEL and S % ATTN_TILE == 0 and w_in.shape[0] == 1
    T = B * S
    x2 = x.reshape(T, D)
    row = lambda v: v.reshape(1, -1)

    g_mix2 = row(g_mix[0])
    w_in2 = w_in.reshape(D, w_in.shape[-1])

    *qkv, w_uvgl, w_ba, w_bs, w_o1, w_q, w_kv, w_o2 = _qkv_proj(
        x2, g_mix2, w_in2,
        ((w_in2, 3 * ATTN_W), (w_branch_attn[0], 0), (w_branch_sgu[0], 0), (w_out[0], 0),
         (w_q_cross[0], 0), (w_kv_cross[0], 0), (w_o_cross[0], 0)))
    y_attn = _dilated_attention(qkv, batch=B)

    h, w_gu, w_dn = _mixer(
        x2, y_attn, g_mix2, w_uvgl, row(b_gate[0]), w_sgu_spatial[0], b_sgu_spatial[0],
        row(g_sgu[0]), w_ba, w_bs, w_o1, ((w_gate_up[0], 0), (w_down[0], 0)))

    h = _cross_attention(h, mem.reshape(B * mem.shape[1], D), row(g_mem[0]), w_kv,
                         row(g_cross[0]), w_q, w_o2, batch=B)
    out = _ffn(h, row(g_ffn[0]), w_gu, w_dn, row(g_final))
    return out.reshape(B, S, D)
```

```python
import functools
import math

import jax
import jax.numpy as jnp
import numpy as np
from jax import lax
from jax.experimental import pallas as pl
from jax.experimental.pallas import tpu as pltpu

D_MODEL = 1024
HEAD_DIM = 64
DIL_GROUPS = ((128, 1), (512, 4), (2048, 16))
N_GROUPS = 3
HEADS_PER_GROUP = 4
N_ATTN_HEADS = N_GROUPS * HEADS_PER_GROUP
ATTN_W = N_ATTN_HEADS * HEAD_DIM
GROUP_W = HEADS_PER_GROUP * HEAD_DIM
BLK = 128
LANES = 128
BF16_SUBLANES = 16
MAX_SUBLANE_STRIDE = 4
HALVES = GROUP_W // LANES
N_STATS = 3
SGU_CHUNK = 128
SGU_GROUPS = 4
SGU_W = 512
SGU_GROUP_W = SGU_W // SGU_GROUPS
MEM_HEADS = 4
MEM_HEAD_DIM = 128
MEM_W = MEM_HEADS * MEM_HEAD_DIM
D_FF = 2816
FF_TILE = 256
EPS = 1e-6
LOG2E = math.log2(math.e)

ATTN_TILE = 2048
TM = 1024
TM_CROSS = 2048
SUB_QKV = 512
SUB_MIX = 512
SUB_CROSS = 512
SUB_FFN = 256
VMEM_LIMIT_QKV = 56 * 1024 * 1024
VMEM_LIMIT_ATTN = 50 * 1024 * 1024
VMEM_LIMIT_MIX = 40 * 1024 * 1024
VMEM_LIMIT_CROSS = 40 * 1024 * 1024
VMEM_LIMIT_FFN = 28 * 1024 * 1024

F32 = jnp.float32
BF16 = jnp.bfloat16


def _rms(xf, g):
    r = lax.rsqrt(jnp.mean(xf * xf, axis=-1, keepdims=True) + EPS)
    return xf * r * g


def _alibi_slopes_grouped():
    def pow2(n):
        start = 2.0 ** (-8.0 / n)
        return [start ** (i + 1) for i in range(n)]
    n = N_ATTN_HEADS
    if math.log2(n).is_integer():
        s = pow2(n)
    else:
        c = 2 ** int(math.floor(math.log2(n)))
        s = pow2(c) + pow2(2 * c)[0::2][: n - c]
    s = np.array(sorted(s, reverse=True), dtype=np.float32)
    return s.reshape(N_GROUPS, HEADS_PER_GROUP)


def _attn_bias(gi):
    window, dil = DIL_GROUPS[gi]
    n_back = window // dil
    steps = (np.arange(BLK)[:, None] + BLK) - np.arange(2 * BLK)[None, :]
    band = (steps >= 0) & (steps <= n_back)
    dist = (np.clip(steps, 0, None) * dil).astype(np.float32)
    slopes = _alibi_slopes_grouped()[gi]
    bias = -slopes[:, None, None] * dist[None]
    bias = np.where(band[None], bias.astype(np.float64) * LOG2E, -np.inf).astype(np.float32)
    return bias.reshape(HEADS_PER_GROUP * BLK, 2 * BLK)


def _const_spec(shape):
    nd = len(shape)
    return pl.BlockSpec(shape, lambda *_: (0,) * nd, pipeline_mode=pl.Buffered(1))


def _cast_specs(weights, steps):
    in_specs, out_specs, out_shapes = [], [], []
    for w, col0 in weights:
        rows, cols = w.shape
        assert rows % (steps * BF16_SUBLANES) == 0 and col0 % LANES == 0
        in_specs.append(pl.BlockSpec((rows // steps, cols), lambda i, *_: (i, 0)))
        out_specs.append(pl.BlockSpec((rows // steps, cols - col0), lambda i, *_: (i, 0)))
        out_shapes.append(jax.ShapeDtypeStruct((rows, cols - col0), BF16))
    return in_specs, out_specs, out_shapes


def _cast_slabs(in_refs, out_refs):
    for src, dst in zip(in_refs, out_refs):
        col0 = src.shape[1] - dst.shape[1]
        dst[...] = src[:, col0:].astype(BF16)


def _residue_rows(src, tmp, dil):
    n = src.shape[0]
    if dil <= MAX_SUBLANE_STRIDE:
        for r in range(dil):
            yield r, (src[pl.ds(r, n // dil, stride=dil), :] if dil > 1 else src[...])
        return
    outer = dil // MAX_SUBLANE_STRIDE
    assert outer <= MAX_SUBLANE_STRIDE
    for r0 in range(MAX_SUBLANE_STRIDE):
        tmp[r0] = src[pl.ds(r0, n // MAX_SUBLANE_STRIDE, stride=MAX_SUBLANE_STRIDE), :]
        for r1 in range(outer):
            yield r1 * MAX_SUBLANE_STRIDE + r0, tmp[r0, pl.ds(r1, n // dil, stride=outer), :]


def _qkv_kernel(x_ref, g_ref, wf_ref, *refs, n_cast):
    cast_in, refs = refs[:n_cast], refs[n_cast:]
    out_refs, cast_out = refs[:3 * N_GROUPS], refs[3 * N_GROUPS:3 * N_GROUPS + n_cast]
    w_ref, scr, scr2 = refs[-3:]

    @pl.when(pl.program_id(0) == 0)
    def _():
        w_ref[...] = wf_ref[...].astype(BF16)

    _cast_slabs(cast_in, cast_out)
    for part in range(TM // SUB_QKV):
        rows = slice(part * SUB_QKV, (part + 1) * SUB_QKV)
        a = _rms(x_ref[rows, :], g_ref[...]).astype(BF16)
        res = jnp.dot(a, w_ref[...], preferred_element_type=F32)
        for which in range(3):
            for gi, (_, dil) in enumerate(DIL_GROUPS):
                slab = (which * ATTN_W + gi * GROUP_W) // LANES
                out = out_refs[gi * 3 + which]
                out_rows = slice(part * SUB_QKV // dil, (part + 1) * SUB_QKV // dil)
                for half in range(HALVES):
                    cols = slice((slab + half) * LANES, (slab + half + 1) * LANES)
                    if dil == 1:
                        pieces = [(0, res[:, cols])]
                    else:
                        src = scr.at[part, slab + half]
                        src[...] = res[:, cols]
                        pieces = _residue_rows(src, scr2.at[part, which * HALVES + half], dil)
                    for r, val in pieces:
                        if which == 0:
                            val = val * (HEAD_DIM ** -0.5 * LOG2E)
                        lo = r * GROUP_W + half * LANES
                        out[out_rows, lo:lo + LANES] = val.astype(BF16)


def _qkv_proj(x2, g_mix, w_in, cast_weights):
    T = x2.shape[0]
    steps = T // TM
    out_shapes, out_specs = [], []
    for _, dil in DIL_GROUPS:
        for _ in range(3):
            out_shapes.append(jax.ShapeDtypeStruct((T // dil, dil * GROUP_W), BF16))
            out_specs.append(pl.BlockSpec((TM // dil, dil * GROUP_W), lambda i: (i, 0)))
    cast_in, cast_out, cast_shapes = _cast_specs(cast_weights, steps)
    return pl.pallas_call(
        functools.partial(_qkv_kernel, n_cast=len(cast_weights)),
        grid=(steps,),
        in_specs=[pl.BlockSpec((TM, D_MODEL), lambda i: (i, 0)),
                  _const_spec((1, D_MODEL)),
                  _const_spec((D_MODEL, 3 * ATTN_W))] + cast_in,
        out_specs=out_specs + cast_out,
        out_shape=out_shapes + cast_shapes,
        scratch_shapes=[pltpu.VMEM((D_MODEL, 3 * ATTN_W), BF16),
                        pltpu.VMEM((TM // SUB_QKV, 3 * ATTN_W // LANES, SUB_QKV, LANES), F32),
                        pltpu.VMEM((TM // SUB_QKV, 3 * HALVES, MAX_SUBLANE_STRIDE,
                                    SUB_QKV // MAX_SUBLANE_STRIDE, LANES), F32)],
        compiler_params=pltpu.CompilerParams(
            dimension_semantics=("arbitrary",), vmem_limit_bytes=VMEM_LIMIT_QKV),
        name="qkv_proj",
    )(x2, g_mix, w_in, *[w for w, _ in cast_weights])


def _attn_block(q, k2, v2, bias, no_prev, head_masks, low_lanes):
    zero = jnp.zeros_like(q)
    qs = jnp.concatenate([jnp.where(m, q, zero) for m in head_masks], axis=0)
    s = lax.dot_general(qs, k2, (((1,), (1,)), ((), ())),
                        preferred_element_type=F32) + bias
    if no_prev is not None:
        s = jnp.where(no_prev, -jnp.inf, s)
    mx = jnp.max(s, axis=-1, keepdims=True)
    e = jnp.exp2(s - mx)
    den = jnp.sum(e, axis=-1, keepdims=True)
    pv = jnp.dot(e.astype(BF16), v2, preferred_element_type=F32)

    def head_rows(x, h):
        return x[h * BLK:(h + 1) * BLK]

    acc, m_rep, l_rep = [], [], []
    for half in range(HALVES):
        ha, hb = 2 * half, 2 * half + 1
        cols = slice(half * LANES, (half + 1) * LANES)
        acc.append(jnp.where(low_lanes, head_rows(pv, ha)[:, cols], head_rows(pv, hb)[:, cols]))
        m_rep.append(jnp.where(low_lanes, head_rows(mx, ha), head_rows(mx, hb)))
        l_rep.append(jnp.where(low_lanes, head_rows(den, ha), head_rows(den, hb)))
    return acc, m_rep, l_rep


def _attn_kernel(*refs):
    in_refs, (y_ref, nat, stage) = refs[:6 * N_GROUPS], refs[6 * N_GROUPS:]
    first_tile = pl.program_id(1) == 0
    lane = lax.broadcasted_iota(jnp.int32, (1, GROUP_W), 1)
    head_masks = [(lane >= HEAD_DIM * h) & (lane < HEAD_DIM * (h + 1))
                  for h in range(HEADS_PER_GROUP)]
    low_lanes = lax.broadcasted_iota(jnp.int32, (1, LANES), 1) < HEAD_DIM
    key_col = lax.broadcasted_iota(jnp.int32, (1, 2 * BLK), 1)
    no_prev_first = jnp.logical_and(first_tile, key_col < BLK)

    def group_blocks(gi):
        q_ref, k_ref, kp_ref, v_ref, vp_ref, bias_ref = in_refs[6 * gi:6 * gi + 6]
        dil = DIL_GROUPS[gi][1]
        for r in range(dil):
            lanes = slice(r * GROUP_W, (r + 1) * GROUP_W)
            for i in range(ATTN_TILE // dil // BLK):
                cur = slice(i * BLK, (i + 1) * BLK)
                if i == 0:
                    kp, vp = kp_ref[:, lanes], vp_ref[:, lanes]
                else:
                    prev = slice((i - 1) * BLK, i * BLK)
                    kp, vp = k_ref[prev, lanes], v_ref[prev, lanes]
                k2 = jnp.concatenate([kp, k_ref[cur, lanes]], axis=0)
                v2 = jnp.concatenate([vp, v_ref[cur, lanes]], axis=0)
                yield r, i, _attn_block(q_ref[cur, lanes], k2, v2, bias_ref[...],
                                        no_prev_first if i == 0 else None,
                                        head_masks, low_lanes)

    for gi in range(1, N_GROUPS):
        dil = DIL_GROUPS[gi][1]
        if dil <= MAX_SUBLANE_STRIDE:
            for r, i, stats in group_blocks(gi):
                dst = pl.ds(i * BLK * dil + r, BLK, stride=dil)
                for kind, parts in enumerate(stats):
                    for half in range(HALVES):
                        nat[gi - 1, kind, half, dst, :] = parts[half]
            continue
        outer = dil // MAX_SUBLANE_STRIDE
        for r, i, stats in group_blocks(gi):
            r1, r0 = divmod(r, MAX_SUBLANE_STRIDE)
            dst = pl.ds(i * BLK * outer + r1, BLK, stride=outer)
            for kind, parts in enumerate(stats):
                for half in range(HALVES):
                    stage[kind, half, r0, dst, :] = parts[half]
        for kind in range(N_STATS):
            for half in range(HALVES):
                for r0 in range(MAX_SUBLANE_STRIDE):
                    dst = pl.ds(r0, ATTN_TILE // MAX_SUBLANE_STRIDE, stride=MAX_SUBLANE_STRIDE)
                    nat[gi - 1, kind, half, dst, :] = stage[kind, half, r0]

    for r, i, (acc0, m0, l0) in group_blocks(0):
        rows = slice(i * BLK, (i + 1) * BLK)
        for half in range(HALVES):
            accs = [acc0[half]] + [nat[g, 0, half, rows, :] for g in range(N_GROUPS - 1)]
            ms = [m0[half]] + [nat[g, 1, half, rows, :] for g in range(N_GROUPS - 1)]
            ls = [l0[half]] + [nat[g, 2, half, rows, :] for g in range(N_GROUPS - 1)]
            m_all = jnp.maximum(jnp.maximum(ms[0], ms[1]), ms[2])
            ws = [jnp.exp2(m - m_all) for m in ms]
            num = ws[0] * accs[0] + ws[1] * accs[1] + ws[2] * accs[2]
            den = ws[0] * ls[0] + ws[1] * ls[1] + ws[2] * ls[2]
            y_ref[rows, half * LANES:(half + 1) * LANES] = (num * (1.0 / den)).astype(BF16)


def _dilated_attention(qkv, batch):
    T = qkv[0].shape[0]
    tiles_per_batch = T // ATTN_TILE // batch

    def cur_map(b, t):
        return (b * tiles_per_batch + t, 0)

    operands, in_specs = [], []
    for gi, (_, dil) in enumerate(DIL_GROUPS):
        qd, kd, vd = qkv[3 * gi:3 * gi + 3]
        rows, width = ATTN_TILE // dil, dil * GROUP_W
        prev_blocks = rows // BLK

        def prev_map(b, t, prev_blocks=prev_blocks):
            return (jnp.maximum((b * tiles_per_batch + t) * prev_blocks - 1, 0), 0)

        bias = jnp.asarray(_attn_bias(gi))
        operands += [qd, kd, kd, vd, vd, bias]
        in_specs += [pl.BlockSpec((rows, width), cur_map),
                     pl.BlockSpec((rows, width), cur_map),
                     pl.BlockSpec((BLK, width), prev_map),
                     pl.BlockSpec((rows, width), cur_map),
                     pl.BlockSpec((BLK, width), prev_map),
                     _const_spec(bias.shape)]
    return pl.pallas_call(
        _attn_kernel,
        grid=(batch, tiles_per_batch),
        in_specs=in_specs,
        out_specs=pl.BlockSpec((ATTN_TILE, GROUP_W), cur_map),
        out_shape=jax.ShapeDtypeStruct((T, GROUP_W), BF16),
        scratch_shapes=[pltpu.VMEM((N_GROUPS - 1, N_STATS, HALVES, ATTN_TILE, LANES), F32),
                        pltpu.VMEM((N_STATS, HALVES, MAX_SUBLANE_STRIDE,
                                    ATTN_TILE // MAX_SUBLANE_STRIDE, LANES), F32)],
        compiler_params=pltpu.CompilerParams(
            dimension_semantics=("arbitrary", "arbitrary"), vmem_limit_bytes=VMEM_LIMIT_ATTN),
        name="dilated_attn",
    )(*operands)


def _mixer_kernel(x_ref, y_ref, g_ref, w_ref, bg_ref, ws_ref, bs_ref, gs_ref,
                  wba_ref, wbs_ref, wo_ref, *refs, n_cast):
    cast_in, h_ref, cast_out = refs[:n_cast], refs[n_cast], refs[n_cast + 1:]
    for j in range(D_FF // FF_TILE):
        for half in range(2):
            src_cols = slice(half * D_FF + j * FF_TILE, half * D_FF + (j + 1) * FF_TILE)
            dst_cols = slice((2 * j + half) * FF_TILE, (2 * j + half + 1) * FF_TILE)
            cast_out[0][:, dst_cols] = cast_in[0][:, src_cols].astype(BF16)
    _cast_slabs(cast_in[1:], cast_out[1:])
    row = lax.broadcasted_iota(jnp.int32, (SGU_CHUNK, SGU_CHUNK), 0)
    col = lax.broadcasted_iota(jnp.int32, (SGU_CHUNK, SGU_CHUNK), 1)
    ws = [jnp.where(row >= col, ws_ref[g], 0.0).astype(BF16) for g in range(SGU_GROUPS)]
    bias = jnp.concatenate(
        [jnp.broadcast_to(jnp.sum(jnp.where(row == col, bs_ref[g:g + 1, :], 0.0),
                                  axis=1, keepdims=True), (SGU_CHUNK, SGU_GROUP_W))
         for g in range(SGU_GROUPS)], axis=1)
    for part in range(TM // SUB_MIX):
        rows = slice(part * SUB_MIX, (part + 1) * SUB_MIX)
        x = x_ref[rows, :]
        a = _rms(x, g_ref[...]).astype(BF16)
        proj = jnp.dot(a, w_ref[...], preferred_element_type=F32)

        z = jax.nn.gelu(proj[:, :2 * SGU_W], approximate=True)
        u = z[:, :SGU_W]
        v = _rms(z[:, SGU_W:], gs_ref[...]).astype(BF16)
        n_chunks = SUB_MIX // SGU_CHUNK
        mixed = []
        for g in range(SGU_GROUPS):
            gl = slice(g * SGU_GROUP_W, (g + 1) * SGU_GROUP_W)
            vg = jnp.concatenate([v[c * SGU_CHUNK:(c + 1) * SGU_CHUNK, gl]
                                  for c in range(n_chunks)], axis=1)
            mixed.append(jnp.dot(ws[g], vg, preferred_element_type=F32))
        chunks = [jnp.concatenate([m[:, c * SGU_GROUP_W:(c + 1) * SGU_GROUP_W] for m in mixed],
                                  axis=1) + bias for c in range(n_chunks)]
        y_sgu = u * jnp.concatenate(chunks, axis=0)

        gates = 1.0 / (1.0 + jnp.exp(-(proj[:, 2 * SGU_W:] + bg_ref[...])))
        merged = (gates[:, :D_MODEL] * jnp.dot(y_ref[rows, :], wba_ref[...],
                                                preferred_element_type=F32)
                  + gates[:, D_MODEL:] * jnp.dot(y_sgu.astype(BF16), wbs_ref[...],
                                                 preferred_element_type=F32))
        h_ref[rows, :] = x + jnp.dot(merged.astype(BF16), wo_ref[...],
                                     preferred_element_type=F32)


def _mixer(x2, y_attn, g_mix, w_uvgl, b_gate, w_s, b_s, g_sgu, w_ba, w_bs, w_out, cast_weights):
    T = x2.shape[0]
    steps = T // TM
    row_spec = lambda w: pl.BlockSpec((TM, w), lambda i: (i, 0))
    consts = (g_mix, w_uvgl, b_gate, w_s, b_s, g_sgu, w_ba, w_bs, w_out)
    const_specs = [pl.BlockSpec(memory_space=pltpu.VMEM) if c.dtype == BF16
                   else _const_spec(c.shape) for c in consts]
    cast_in, cast_out, cast_shapes = _cast_specs(cast_weights, steps)
    return pl.pallas_call(
        functools.partial(_mixer_kernel, n_cast=len(cast_weights)),
        grid=(steps,),
        in_specs=[row_spec(D_MODEL), row_spec(GROUP_W)] + const_specs + cast_in,
        out_specs=[row_spec(D_MODEL)] + cast_out,
        out_shape=[jax.ShapeDtypeStruct((T, D_MODEL), F32)] + cast_shapes,
        compiler_params=pltpu.CompilerParams(
            dimension_semantics=("arbitrary",), vmem_limit_bytes=VMEM_LIMIT_MIX),
        name="mixer_tail",
    )(x2, y_attn, *consts, *[w for w, _ in cast_weights])


def _cross_kernel(h_ref, mem_ref, gm_ref, wkv_ref, g_ref, wq_ref, wo_ref, out_ref, kv_ref):
    @pl.when(pl.program_id(1) == 0)
    def _():
        m = _rms(mem_ref[...], gm_ref[...]).astype(BF16)
        kv_ref[...] = jnp.dot(m, wkv_ref[...], preferred_element_type=F32).astype(BF16)

    for part in range(TM_CROSS // SUB_CROSS):
        rows = slice(part * SUB_CROSS, (part + 1) * SUB_CROSS)
        h = h_ref[rows, :]
        c = _rms(h, g_ref[...]).astype(BF16)
        q = jnp.dot(c, wq_ref[...], preferred_element_type=F32).astype(BF16)
        heads = []
        for hd in range(MEM_HEADS):
            cols = slice(hd * MEM_HEAD_DIM, (hd + 1) * MEM_HEAD_DIM)
            k = kv_ref[:, cols]
            v = kv_ref[:, MEM_W + hd * MEM_HEAD_DIM:MEM_W + (hd + 1) * MEM_HEAD_DIM]
            s = lax.dot_general(q[:, cols], k, (((1,), (1,)), ((), ())),
                                preferred_element_type=F32) * (MEM_HEAD_DIM ** -0.5)
            e = jnp.exp(s - jnp.max(s, axis=-1, keepdims=True))
            den = jnp.sum(e, axis=-1, keepdims=True)
            o = jnp.dot(e.astype(BF16), v, preferred_element_type=F32) * (1.0 / den)
            heads.append(o.astype(BF16))
        o = jnp.concatenate(heads, axis=1)
        out_ref[rows, :] = h + jnp.dot(o, wo_ref[...], preferred_element_type=F32)


def _cross_attention(h, mem2, g_mem, w_kv, g_cross, w_q, w_o, batch):
    T = h.shape[0]
    blocks_per_batch = T // batch // TM_CROSS
    mem_len = mem2.shape[0] // batch
    consts = (g_mem, w_kv, g_cross, w_q, w_o)
    row_map = lambda b, i: (b * blocks_per_batch + i, 0)
    return pl.pallas_call(
        _cross_kernel,
        grid=(batch, blocks_per_batch),
        in_specs=[pl.BlockSpec((TM_CROSS, D_MODEL), row_map),
                  pl.BlockSpec((mem_len, D_MODEL), lambda b, i: (b, 0))]
                 + [_const_spec(c.shape) for c in consts],
        out_specs=pl.BlockSpec((TM_CROSS, D_MODEL), row_map),
        out_shape=jax.ShapeDtypeStruct((T, D_MODEL), F32),
        scratch_shapes=[pltpu.VMEM((mem_len, 2 * MEM_W), BF16)],
        compiler_params=pltpu.CompilerParams(
            dimension_semantics=("arbitrary", "arbitrary"), vmem_limit_bytes=VMEM_LIMIT_CROSS),
        name="cross_attn",
    )(h, mem2, *consts)


def _ffn_kernel(h_ref, g_ref, wgu_ref, wd_ref, gf_ref, out_ref):
    for part in range(TM // SUB_FFN):
        rows = slice(part * SUB_FFN, (part + 1) * SUB_FFN)
        h = h_ref[rows, :]
        f = _rms(h, g_ref[...]).astype(BF16)
        gu = jnp.dot(f, wgu_ref[...], preferred_element_type=F32)
        tiles = []
        for j in range(D_FF // FF_TILE):
            gt = gu[:, 2 * j * FF_TILE:(2 * j + 1) * FF_TILE]
            up = gu[:, (2 * j + 1) * FF_TILE:(2 * j + 2) * FF_TILE]
            tiles.append(gt * (1.0 / (1.0 + jnp.exp(-gt))) * up)
        act = jnp.concatenate(tiles, axis=1).astype(BF16)
        acc = h + jnp.dot(act, wd_ref[...], preferred_element_type=F32)
        out_ref[rows, :] = _rms(acc, gf_ref[...])


def _ffn(h, g_ffn, w_gu, w_down, g_final):
    T = h.shape[0]
    consts = (g_ffn, w_gu, w_down, g_final)
    return pl.pallas_call(
        _ffn_kernel,
        grid=(T // TM,),
        in_specs=[pl.BlockSpec((TM, D_MODEL), lambda i: (i, 0)),
                  _const_spec(g_ffn.shape),
                  pl.BlockSpec(memory_space=pltpu.VMEM),
                  pl.BlockSpec(memory_space=pltpu.VMEM),
                  _const_spec(g_final.shape)],
        out_specs=pl.BlockSpec((TM, D_MODEL), lambda i: (i, 0)),
        out_shape=jax.ShapeDtypeStruct((T, D_MODEL), F32),
        compiler_params=pltpu.CompilerParams(
            dimension_semantics=("arbitrary",), vmem_limit_bytes=VMEM_LIMIT_FFN),
        name="ffn",
    )(h, *consts)


def kernel(x, mem, g_mix, w_in, b_gate, w_sgu_spatial, b_sgu_spatial, g_sgu, w_branch_attn,
           w_branch_sgu, w_out, g_cross, g_mem, w_q_cross, w_kv_cross, w_o_cross, g_ffn,
           w_gate_up, w_down, g_final):
    B, S, D = x.shape
    assert D == D_MODEL and S % ATTN_TILE == 0 and w_in.shape[0] == 1
    T = B * S
    x2 = x.reshape(T, D)
    row = lambda v: v.reshape(1, -1)

    g_mix2 = row(g_mix[0])
    w_in2 = w_in.reshape(D, w_in.shape[-1])

    *qkv, w_uvgl, w_ba, w_bs, w_o1, w_q, w_kv, w_o2 = _qkv_proj(
        x2, g_mix2, w_in2,
        ((w_in2, 3 * ATTN_W), (w_branch_attn[0], 0), (w_branch_sgu[0], 0), (w_out[0], 0),
         (w_q_cross[0], 0), (w_kv_cross[0], 0), (w_o_cross[0], 0)))
    y_attn = _dilated_attention(qkv, batch=B)

    h, w_gu, w_dn = _mixer(
        x2, y_attn, g_mix2, w_uvgl, row(b_gate[0]), w_sgu_spatial[0], b_sgu_spatial[0],
        row(g_sgu[0]), w_ba, w_bs, w_o1, ((w_gate_up[0], 0), (w_down[0], 0)))

    h = _cross_attention(h, mem.reshape(B * mem.shape[1], D), row(g_mem[0]), w_kv,
                         row(g_cross[0]), w_q, w_o2, batch=B)
    out = _ffn(h, row(g_ffn[0]), w_gu, w_dn, row(g_final))
    return out.reshape(B, S, D)
```

```python
import functools
import math

import jax
import jax.numpy as jnp
import numpy as np
from jax import lax
from jax.experimental import pallas as pl
from jax.experimental.pallas import tpu as pltpu

D_MODEL = 1024
HEAD_DIM = 64
DIL_GROUPS = ((128, 1), (512, 4), (2048, 16))
N_GROUPS = 3
HEADS_PER_GROUP = 4
N_ATTN_HEADS = N_GROUPS * HEADS_PER_GROUP
ATTN_W = N_ATTN_HEADS * HEAD_DIM
GROUP_W = HEADS_PER_GROUP * HEAD_DIM
BLK = 128
LANES = 128
BF16_SUBLANES = 16
MAX_SUBLANE_STRIDE = 4
HALVES = GROUP_W // LANES
N_STATS = 3
SGU_CHUNK = 128
SGU_GROUPS = 4
SGU_W = 512
SGU_GROUP_W = SGU_W // SGU_GROUPS
MEM_HEADS = 4
MEM_HEAD_DIM = 128
MEM_W = MEM_HEADS * MEM_HEAD_DIM
D_FF = 2816
FF_TILE = 256
EPS = 1e-6
LOG2E = math.log2(math.e)

ATTN_TILE = 2048
TM = 1024
TM_CROSS = 2048
SUB_QKV = 512
SUB_MIX = 512
SUB_CROSS = 512
SUB_FFN = 256
VMEM_LIMIT_QKV = 56 * 1024 * 1024
VMEM_LIMIT_ATTN = 50 * 1024 * 1024
VMEM_LIMIT_MIX = 40 * 1024 * 1024
VMEM_LIMIT_CROSS = 40 * 1024 * 1024
VMEM_LIMIT_FFN = 28 * 1024 * 1024

F32 = jnp.float32
BF16 = jnp.bfloat16


def _rms(xf, g):
    r = lax.rsqrt(jnp.mean(xf * xf, axis=-1, keepdims=True) + EPS)
    return xf * r * g


def _alibi_slopes_grouped():
    def pow2(n):
        start = 2.0 ** (-8.0 / n)
        return [start ** (i + 1) for i in range(n)]
    n = N_ATTN_HEADS
    if math.log2(n).is_integer():
        s = pow2(n)
    else:
        c = 2 ** int(math.floor(math.log2(n)))
        s = pow2(c) + pow2(2 * c)[0::2][: n - c]
    s = np.array(sorted(s, reverse=True), dtype=np.float32)
    return s.reshape(N_GROUPS, HEADS_PER_GROUP)


def _attn_bias(gi):
    window, dil = DIL_GROUPS[gi]
    n_back = window // dil
    steps = (np.arange(BLK)[:, None] + BLK) - np.arange(2 * BLK)[None, :]
    band = (steps >= 0) & (steps <= n_back)
    dist = (np.clip(steps, 0, None) * dil).astype(np.float32)
    slopes = _alibi_slopes_grouped()[gi]
    bias = -slopes[:, None, None] * dist[None]
    bias = np.where(band[None], bias.astype(np.float64) * LOG2E, -np.inf).astype(np.float32)
    return bias.reshape(HEADS_PER_GROUP * BLK, 2 * BLK)


def _const_spec(shape):
    nd = len(shape)
    return pl.BlockSpec(shape, lambda *_: (0,) * nd, pipeline_mode=pl.Buffered(1))


def _cast_specs(weights, steps):
    in_specs, out_specs, out_shapes = [], [], []
    for w, col0 in weights:
        rows, cols = w.shape
        assert rows % (steps * BF16_SUBLANES) == 0 and col0 % LANES == 0
        in_specs.append(pl.BlockSpec((rows // steps, cols), lambda i, *_: (i, 0)))
        out_specs.append(pl.BlockSpec((rows // steps, cols - col0), lambda i, *_: (i, 0)))
        out_shapes.append(jax.ShapeDtypeStruct((rows, cols - col0), BF16))
    return in_specs, out_specs, out_shapes


def _cast_slabs(in_refs, out_refs):
    for src, dst in zip(in_refs, out_refs):
        col0 = src.shape[1] - dst.shape[1]
        dst[...] = src[:, col0:].astype(BF16)


def _residue_rows(src, tmp, dil):
    n = src.shape[0]
    if dil <= MAX_SUBLANE_STRIDE:
        for r in range(dil):
            yield r, (src[pl.ds(r, n // dil, stride=dil), :] if dil > 1 else src[...])
        return
    outer = dil // MAX_SUBLANE_STRIDE
    assert outer <= MAX_SUBLANE_STRIDE
    for r0 in range(MAX_SUBLANE_STRIDE):
        tmp[r0] = src[pl.ds(r0, n // MAX_SUBLANE_STRIDE, stride=MAX_SUBLANE_STRIDE), :]
        for r1 in range(outer):
            yield r1 * MAX_SUBLANE_STRIDE + r0, tmp[r0, pl.ds(r1, n // dil, stride=outer), :]


def _qkv_kernel(x_ref, g_ref, wf_ref, *refs, n_cast):
    cast_in, refs = refs[:n_cast], refs[n_cast:]
    out_refs, cast_out = refs[:3 * N_GROUPS], refs[3 * N_GROUPS:3 * N_GROUPS + n_cast]
    w_ref, scr, scr2 = refs[-3:]

    @pl.when(pl.program_id(0) == 0)
    def _():
        w_ref[...] = wf_ref[...].astype(BF16)

    _cast_slabs(cast_in, cast_out)
    for part in range(TM // SUB_QKV):
        rows = slice(part * SUB_QKV, (part + 1) * SUB_QKV)
        a = _rms(x_ref[rows, :], g_ref[...]).astype(BF16)
        res = jnp.dot(a, w_ref[...], preferred_element_type=F32)
        for which in range(3):
            for gi, (_, dil) in enumerate(DIL_GROUPS):
                slab = (which * ATTN_W + gi * GROUP_W) // LANES
                out = out_refs[gi * 3 + which]
                out_rows = slice(part * SUB_QKV // dil, (part + 1) * SUB_QKV // dil)
                for half in range(HALVES):
                    cols = slice((slab + half) * LANES, (slab + half + 1) * LANES)
                    if dil == 1:
                        pieces = [(0, res[:, cols])]
                    else:
                        src = scr.at[part, slab + half]
                        src[...] = res[:, cols]
                        pieces = _residue_rows(src, scr2.at[part, which * HALVES + half], dil)
                    for r, val in pieces:
                        if which == 0:
                            val = val * (HEAD_DIM ** -0.5 * LOG2E)
                        lo = r * GROUP_W + half * LANES
                        out[out_rows, lo:lo + LANES] = val.astype(BF16)


def _qkv_proj(x2, g_mix, w_in, cast_weights):
    T = x2.shape[0]
    steps = T // TM
    out_shapes, out_specs = [], []
    for _, dil in DIL_GROUPS:
        for _ in range(3):
            out_shapes.append(jax.ShapeDtypeStruct((T // dil, dil * GROUP_W), BF16))
            out_specs.append(pl.BlockSpec((TM // dil, dil * GROUP_W), lambda i: (i, 0)))
    cast_in, cast_out, cast_shapes = _cast_specs(cast_weights, steps)
    return pl.pallas_call(
        functools.partial(_qkv_kernel, n_cast=len(cast_weights)),
        grid=(steps,),
        in_specs=[pl.BlockSpec((TM, D_MODEL), lambda i: (i, 0)),
                  _const_spec((1, D_MODEL)),
                  _const_spec((D_MODEL, 3 * ATTN_W))] + cast_in,
        out_specs=out_specs + cast_out,
        out_shape=out_shapes + cast_shapes,
        scratch_shapes=[pltpu.VMEM((D_MODEL, 3 * ATTN_W), BF16),
                        pltpu.VMEM((TM // SUB_QKV, 3 * ATTN_W // LANES, SUB_QKV, LANES), F32),
                        pltpu.VMEM((TM // SUB_QKV, 3 * HALVES, MAX_SUBLANE_STRIDE,
                                    SUB_QKV // MAX_SUBLANE_STRIDE, LANES), F32)],
        compiler_params=pltpu.CompilerParams(
            dimension_semantics=("arbitrary",), vmem_limit_bytes=VMEM_LIMIT_QKV),
        name="qkv_proj",
    )(x2, g_mix, w_in, *[w for w, _ in cast_weights])


def _attn_block(q, k2, v2, bias, no_prev, head_masks, low_lanes):
    zero = jnp.zeros_like(q)
    qs = jnp.concatenate([jnp.where(m, q, zero) for m in head_masks], axis=0)
    s = lax.dot_general(qs, k2, (((1,), (1,)), ((), ())),
                        preferred_element_type=F32) + bias
    if no_prev is not None:
        s = jnp.where(no_prev, -jnp.inf, s)
    mx = jnp.max(s, axis=-1, keepdims=True)
    e = jnp.exp2(s - mx)
    den = jnp.sum(e, axis=-1, keepdims=True)
    pv = jnp.dot(e.astype(BF16), v2, preferred_element_type=F32)

    def head_rows(x, h):
        return x[h * BLK:(h + 1) * BLK]

    acc, m_rep, l_rep = [], [], []
    for half in range(HALVES):
        ha, hb = 2 * half, 2 * half + 1
        cols = slice(half * LANES, (half + 1) * LANES)
        acc.append(jnp.where(low_lanes, head_rows(pv, ha)[:, cols], head_rows(pv, hb)[:, cols]))
        m_rep.append(jnp.where(low_lanes, head_rows(mx, ha), head_rows(mx, hb)))
        l_rep.append(jnp.where(low_lanes, head_rows(den, ha), head_rows(den, hb)))
    return acc, m_rep, l_rep


def _attn_kernel(*refs):
    in_refs, (y_ref, nat, stage) = refs[:6 * N_GROUPS], refs[6 * N_GROUPS:]
    first_tile = pl.program_id(1) == 0
    lane = lax.broadcasted_iota(jnp.int32, (1, GROUP_W), 1)
    head_masks = [(lane >= HEAD_DIM * h) & (lane < HEAD_DIM * (h + 1))
                  for h in range(HEADS_PER_GROUP)]
    low_lanes = lax.broadcasted_iota(jnp.int32, (1, LANES), 1) < HEAD_DIM
    key_col = lax.broadcasted_iota(jnp.int32, (1, 2 * BLK), 1)
    no_prev_first = jnp.logical_and(first_tile, key_col < BLK)

    def group_blocks(gi):
        q_ref, k_ref, kp_ref, v_ref, vp_ref, bias_ref = in_refs[6 * gi:6 * gi + 6]
        dil = DIL_GROUPS[gi][1]
        for r in range(dil):
            lanes = slice(r * GROUP_W, (r + 1) * GROUP_W)
            for i in range(ATTN_TILE // dil // BLK):
                cur = slice(i * BLK, (i + 1) * BLK)
                if i == 0:
                    kp, vp = kp_ref[:, lanes], vp_ref[:, lanes]
                else:
                    prev = slice((i - 1) * BLK, i * BLK)
                    kp, vp = k_ref[prev, lanes], v_ref[prev, lanes]
                k2 = jnp.concatenate([kp, k_ref[cur, lanes]], axis=0)
                v2 = jnp.concatenate([vp, v_ref[cur, lanes]], axis=0)
                yield r, i, _attn_block(q_ref[cur, lanes], k2, v2, bias_ref[...],
                                        no_prev_first if i == 0 else None,
                                        head_masks, low_lanes)

    for gi in range(1, N_GROUPS):
        dil = DIL_GROUPS[gi][1]
        if dil <= MAX_SUBLANE_STRIDE:
            for r, i, stats in group_blocks(gi):
                dst = pl.ds(i * BLK * dil + r, BLK, stride=dil)
                for kind, parts in enumerate(stats):
                    for half in range(HALVES):
                        nat[gi - 1, kind, half, dst, :] = parts[half]
            continue
        outer = dil // MAX_SUBLANE_STRIDE
        for r, i, stats in group_blocks(gi):
            r1, r0 = divmod(r, MAX_SUBLANE_STRIDE)
            dst = pl.ds(i * BLK * outer + r1, BLK, stride=outer)
            for kind, parts in enumerate(stats):
                for half in range(HALVES):
                    stage[kind, half, r0, dst, :] = parts[half]
        for kind in range(N_STATS):
            for half in range(HALVES):
                for r0 in range(MAX_SUBLANE_STRIDE):
                    dst = pl.ds(r0, ATTN_TILE // MAX_SUBLANE_STRIDE, stride=MAX_SUBLANE_STRIDE)
                    nat[gi - 1, kind, half, dst, :] = stage[kind, half, r0]

    for r, i, (acc0, m0, l0) in group_blocks(0):
        rows = slice(i * BLK, (i + 1) * BLK)
        for half in range(HALVES):
            accs = [acc0[half]] + [nat[g, 0, half, rows, :] for g in range(N_GROUPS - 1)]
            ms = [m0[half]] + [nat[g, 1, half, rows, :] for g in range(N_GROUPS - 1)]
            ls = [l0[half]] + [nat[g, 2, half, rows, :] for g in range(N_GROUPS - 1)]
            m_all = jnp.maximum(jnp.maximum(ms[0], ms[1]), ms[2])
            ws = [jnp.exp2(m - m_all) for m in ms]
            num = ws[0] * accs[0] + ws[1] * accs[1] + ws[2] * accs[2]
            den = ws[0] * ls[0] + ws[1] * ls[1] + ws[2] * ls[2]
            y_ref[rows, half * LANES:(half + 1) * LANES] = (num * (1.0 / den)).astype(BF16)


def _dilated_attention(qkv, batch):
    T = qkv[0].shape[0]
    tiles_per_batch = T // ATTN_TILE // batch

    def cur_map(b, t):
        return (b * tiles_per_batch + t, 0)

    operands, in_specs = [], []
    for gi, (_, dil) in enumerate(DIL_GROUPS):
        qd, kd, vd = qkv[3 * gi:3 * gi + 3]
        rows, width = ATTN_TILE // dil, dil * GROUP_W
        prev_blocks = rows // BLK

        def prev_map(b, t, prev_blocks=prev_blocks):
            return (jnp.maximum((b * tiles_per_batch + t) * prev_blocks - 1, 0), 0)

        bias = jnp.asarray(_attn_bias(gi))
        operands += [qd, kd, kd, vd, vd, bias]
        in_specs += [pl.BlockSpec((rows, width), cur_map),
                     pl.BlockSpec((rows, width), cur_map),
                     pl.BlockSpec((BLK, width), prev_map),
                     pl.BlockSpec((rows, width), cur_map),
                     pl.BlockSpec((BLK, width), prev_map),
                     _const_spec(bias.shape)]
    return pl.pallas_call(
        _attn_kernel,
        grid=(batch, tiles_per_batch),
        in_specs=in_specs,
        out_specs=pl.BlockSpec((ATTN_TILE, GROUP_W), cur_map),
        out_shape=jax.ShapeDtypeStruct((T, GROUP_W), BF16),
        scratch_shapes=[pltpu.VMEM((N_GROUPS - 1, N_STATS, HALVES, ATTN_TILE, LANES), F32),
                        pltpu.VMEM((N_STATS, HALVES, MAX_SUBLANE_STRIDE,
                                    ATTN_TILE // MAX_SUBLANE_STRIDE, LANES), F32)],
        compiler_params=pltpu.CompilerParams(
            dimension_semantics=("arbitrary", "arbitrary"), vmem_limit_bytes=VMEM_LIMIT_ATTN),
        name="dilated_attn",
    )(*operands)


def _mixer_kernel(x_ref, y_ref, g_ref, w_ref, bg_ref, ws_ref, bs_ref, gs_ref,
                  wba_ref, wbs_ref, wo_ref, *refs, n_cast):
    cast_in, h_ref, cast_out = refs[:n_cast], refs[n_cast], refs[n_cast + 1:]
    for j in range(D_FF // FF_TILE):
        for half in range(2):
            src_cols = slice(half * D_FF + j * FF_TILE, half * D_FF + (j + 1) * FF_TILE)
            dst_cols = slice((2 * j + half) * FF_TILE, (2 * j + half + 1) * FF_TILE)
            cast_out[0][:, dst_cols] = cast_in[0][:, src_cols].astype(BF16)
    _cast_slabs(cast_in[1:], cast_out[1:])
    row = lax.broadcasted_iota(jnp.int32, (SGU_CHUNK, SGU_CHUNK), 0)
    col = lax.broadcasted_iota(jnp.int32, (SGU_CHUNK, SGU_CHUNK), 1)
    ws = [jnp.where(row >= col, ws_ref[g], 0.0).astype(BF16) for g in range(SGU_GROUPS)]
    bias = jnp.concatenate(
        [jnp.broadcast_to(jnp.sum(jnp.where(row == col, bs_ref[g:g + 1, :], 0.0),
                                  axis=1, keepdims=True), (SGU_CHUNK, SGU_GROUP_W))
         for g in range(SGU_GROUPS)], axis=1)
    for part in range(TM // SUB_MIX):
        rows = slice(part * SUB_MIX, (part + 1) * SUB_MIX)
        x = x_ref[rows, :]
        a = _rms(x, g_ref[...]).astype(BF16)
        proj = jnp.dot(a, w_ref[...], preferred_element_type=F32)

        z = jax.nn.gelu(proj[:, :2 * SGU_W], approximate=True)
        u = z[:, :SGU_W]
        v = _rms(z[:, SGU_W:], gs_ref[...]).astype(BF16)
        n_chunks = SUB_MIX // SGU_CHUNK
        mixed = []
        for g in range(SGU_GROUPS):
            gl = slice(g * SGU_GROUP_W, (g + 1) * SGU_GROUP_W)
            vg = jnp.concatenate([v[c * SGU_CHUNK:(c + 1) * SGU_CHUNK, gl]
                                  for c in range(n_chunks)], axis=1)
            mixed.append(jnp.dot(ws[g], vg, preferred_element_type=F32))
        chunks = [jnp.concatenate([m[:, c * SGU_GROUP_W:(c + 1) * SGU_GROUP_W] for m in mixed],
                                  axis=1) + bias for c in range(n_chunks)]
        y_sgu = u * jnp.concatenate(chunks, axis=0)

        gates = 0.5 * jnp.tanh(0.5 * (proj[:, 2 * SGU_W:] + bg_ref[...])) + 0.5
        merged = (gates[:, :D_MODEL] * jnp.dot(y_ref[rows, :], wba_ref[...],
                                                preferred_element_type=F32)
                  + gates[:, D_MODEL:] * jnp.dot(y_sgu.astype(BF16), wbs_ref[...],
                                                 preferred_element_type=F32))
        h_ref[rows, :] = x + jnp.dot(merged.astype(BF16), wo_ref[...],
                                     preferred_element_type=F32)


def _mixer(x2, y_attn, g_mix, w_uvgl, b_gate, w_s, b_s, g_sgu, w_ba, w_bs, w_out, cast_weights):
    T = x2.shape[0]
    steps = T // TM
    row_spec = lambda w: pl.BlockSpec((TM, w), lambda i: (i, 0))
    consts = (g_mix, w_uvgl, b_gate, w_s, b_s, g_sgu, w_ba, w_bs, w_out)
    const_specs = [pl.BlockSpec(memory_space=pltpu.VMEM) if c.dtype == BF16
                   else _const_spec(c.shape) for c in consts]
    cast_in, cast_out, cast_shapes = _cast_specs(cast_weights, steps)
    return pl.pallas_call(
        functools.partial(_mixer_kernel, n_cast=len(cast_weights)),
        grid=(steps,),
        in_specs=[row_spec(D_MODEL), row_spec(GROUP_W)] + const_specs + cast_in,
        out_specs=[row_spec(D_MODEL)] + cast_out,
        out_shape=[jax.ShapeDtypeStruct((T, D_MODEL), F32)] + cast_shapes,
        compiler_params=pltpu.CompilerParams(
            dimension_semantics=("arbitrary",), vmem_limit_bytes=VMEM_LIMIT_MIX),
        name="mixer_tail",
    )(x2, y_attn, *consts, *[w for w, _ in cast_weights])


def _cross_kernel(h_ref, mem_ref, gm_ref, wkv_ref, g_ref, wq_ref, wo_ref, out_ref, kv_ref):
    @pl.when(pl.program_id(1) == 0)
    def _():
        m = _rms(mem_ref[...], gm_ref[...]).astype(BF16)
        kv_ref[...] = jnp.dot(m, wkv_ref[...], preferred_element_type=F32).astype(BF16)

    for part in range(TM_CROSS // SUB_CROSS):
        rows = slice(part * SUB_CROSS, (part + 1) * SUB_CROSS)
        h = h_ref[rows, :]
        c = _rms(h, g_ref[...]).astype(BF16)
        q = jnp.dot(c, wq_ref[...], preferred_element_type=F32).astype(BF16)
        heads = []
        for hd in range(MEM_HEADS):
            cols = slice(hd * MEM_HEAD_DIM, (hd + 1) * MEM_HEAD_DIM)
            k = kv_ref[:, cols]
            v = kv_ref[:, MEM_W + hd * MEM_HEAD_DIM:MEM_W + (hd + 1) * MEM_HEAD_DIM]
            s = lax.dot_general(q[:, cols], k, (((1,), (1,)), ((), ())),
                                preferred_element_type=F32) * (MEM_HEAD_DIM ** -0.5)
            e = jnp.exp(s - jnp.max(s, axis=-1, keepdims=True))
            den = jnp.sum(e, axis=-1, keepdims=True)
            o = jnp.dot(e.astype(BF16), v, preferred_element_type=F32) * (1.0 / den)
            heads.append(o.astype(BF16))
        o = jnp.concatenate(heads, axis=1)
        out_ref[rows, :] = h + jnp.dot(o, wo_ref[...], preferred_element_type=F32)


def _cross_attention(h, mem2, g_mem, w_kv, g_cross, w_q, w_o, batch):
    T = h.shape[0]
    blocks_per_batch = T // batch // TM_CROSS
    mem_len = mem2.shape[0] // batch
    consts = (g_mem, w_kv, g_cross, w_q, w_o)
    row_map = lambda b, i: (b * blocks_per_batch + i, 0)
    return pl.pallas_call(
        _cross_kernel,
        grid=(batch, blocks_per_batch),
        in_specs=[pl.BlockSpec((TM_CROSS, D_MODEL), row_map),
                  pl.BlockSpec((mem_len, D_MODEL), lambda b, i: (b, 0))]
                 + [_const_spec(c.shape) for c in consts],
        out_specs=pl.BlockSpec((TM_CROSS, D_MODEL), row_map),
        out_shape=jax.ShapeDtypeStruct((T, D_MODEL), F32),
        scratch_shapes=[pltpu.VMEM((mem_len, 2 * MEM_W), BF16)],
        compiler_params=pltpu.CompilerParams(
            dimension_semantics=("arbitrary", "arbitrary"), vmem_limit_bytes=VMEM_LIMIT_CROSS),
        name="cross_attn",
    )(h, mem2, *consts)


def _ffn_kernel(h_ref, g_ref, wgu_ref, wd_ref, gf_ref, out_ref):
    for part in range(TM // SUB_FFN):
        rows = slice(part * SUB_FFN, (part + 1) * SUB_FFN)
        h = h_ref[rows, :]
        f = _rms(h, g_ref[...]).astype(BF16)
        gu = jnp.dot(f, wgu_ref[...], preferred_element_type=F32)
        tiles = []
        for j in range(D_FF // FF_TILE):
            gt = gu[:, 2 * j * FF_TILE:(2 * j + 1) * FF_TILE]
            up = gu[:, (2 * j + 1) * FF_TILE:(2 * j + 2) * FF_TILE]
            tiles.append((0.5 * gt) * (1.0 + jnp.tanh(0.5 * gt)) * up)
        act = jnp.concatenate(tiles, axis=1).astype(BF16)
        acc = h + jnp.dot(act, wd_ref[...], preferred_element_type=F32)
        out_ref[rows, :] = _rms(acc, gf_ref[...])


def _ffn(h, g_ffn, w_gu, w_down, g_final):
    T = h.shape[0]
    consts = (g_ffn, w_gu, w_down, g_final)
    return pl.pallas_call(
        _ffn_kernel,
        grid=(T // TM,),
        in_specs=[pl.BlockSpec((TM, D_MODEL), lambda i: (i, 0)),
                  _const_spec(g_ffn.shape),
                  pl.BlockSpec(memory_space=pltpu.VMEM),
                  pl.BlockSpec(memory_space=pltpu.VMEM),
                  _const_spec(g_final.shape)],
        out_specs=pl.BlockSpec((TM, D_MODEL), lambda i: (i, 0)),
        out_shape=jax.ShapeDtypeStruct((T, D_MODEL), F32),
        compiler_params=pltpu.CompilerParams(
            dimension_semantics=("arbitrary",), vmem_limit_bytes=VMEM_LIMIT_FFN),
        name="ffn",
    )(h, *consts)


def kernel(x, mem, g_mix, w_in, b_gate, w_sgu_spatial, b_sgu_spatial, g_sgu, w_branch_attn,
           w_branch_sgu, w_out, g_cross, g_mem, w_q_cross, w_kv_cross, w_o_cross, g_ffn,
           w_gate_up, w_down, g_final):
    B, S, D = x.shape
    assert D == D_MODEL and S % ATTN_TILE == 0 and w_in.shape[0] == 1
    T = B * S
    x2 = x.reshape(T, D)
    row = lambda v: v.reshape(1, -1)

    g_mix2 = row(g_mix[0])
    w_in2 = w_in.reshape(D, w_in.shape[-1])

    *qkv, w_uvgl, w_ba, w_bs, w_o1, w_q, w_kv, w_o2 = _qkv_proj(
        x2, g_mix2, w_in2,
        ((w_in2, 3 * ATTN_W), (w_branch_attn[0], 0), (w_branch_sgu[0], 0), (w_out[0], 0),
         (w_q_cross[0], 0), (w_kv_cross[0], 0), (w_o_cross[0], 0)))
    y_attn = _dilated_attention(qkv, batch=B)

    h, w_gu, w_dn = _mixer(
        x2, y_attn, g_mix2, w_uvgl, row(b_gate[0]), w_sgu_spatial[0], b_sgu_spatial[0],
        row(g_sgu[0]), w_ba, w_bs, w_o1, ((w_gate_up[0], 0), (w_down[0], 0)))

    h = _cross_attention(h, mem.reshape(B * mem.shape[1], D), row(g_mem[0]), w_kv,
                         row(g_cross[0]), w_q, w_o2, batch=B)
    out = _ffn(h, row(g_ffn[0]), w_gu, w_dn, row(g_final))
    return out.reshape(B, S, D)
```

```python
import functools
import math

import jax
import jax.numpy as jnp
import numpy as np
from jax import lax
from jax.experimental import pallas as pl
from jax.experimental.pallas import tpu as pltpu

D_MODEL = 1024
HEAD_DIM = 64
DIL_GROUPS = ((128, 1), (512, 4), (2048, 16))
N_GROUPS = 3
HEADS_PER_GROUP = 4
N_ATTN_HEADS = N_GROUPS * HEADS_PER_GROUP
ATTN_W = N_ATTN_HEADS * HEAD_DIM
GROUP_W = HEADS_PER_GROUP * HEAD_DIM
BLK = 128
LANES = 128
BF16_SUBLANES = 16
MAX_SUBLANE_STRIDE = 4
HALVES = GROUP_W // LANES
N_STATS = 3
SGU_CHUNK = 128
SGU_GROUPS = 4
SGU_W = 512
SGU_GROUP_W = SGU_W // SGU_GROUPS
MEM_HEADS = 4
MEM_HEAD_DIM = 128
MEM_W = MEM_HEADS * MEM_HEAD_DIM
D_FF = 2816
FF_TILE = 256
EPS = 1e-6
LOG2E = math.log2(math.e)

ATTN_TILE = 2048
TM = 1024
SUB_QKV = 512
SUB_MIX = 512
SUB_FFN = 256
VMEM_LIMIT_QKV = 56 * 1024 * 1024
VMEM_LIMIT_ATTN = 50 * 1024 * 1024
VMEM_LIMIT_MIX = 40 * 1024 * 1024
VMEM_LIMIT_FFN = 28 * 1024 * 1024

F32 = jnp.float32
BF16 = jnp.bfloat16


def _rms(xf, g):
    r = lax.rsqrt(jnp.mean(xf * xf, axis=-1, keepdims=True) + EPS)
    return xf * r * g


def _alibi_slopes_grouped():
    def pow2(n):
        start = 2.0 ** (-8.0 / n)
        return [start ** (i + 1) for i in range(n)]
    n = N_ATTN_HEADS
    if math.log2(n).is_integer():
        s = pow2(n)
    else:
        c = 2 ** int(math.floor(math.log2(n)))
        s = pow2(c) + pow2(2 * c)[0::2][: n - c]
    s = np.array(sorted(s, reverse=True), dtype=np.float32)
    return s.reshape(N_GROUPS, HEADS_PER_GROUP)


def _attn_bias(gi):
    window, dil = DIL_GROUPS[gi]
    n_back = window // dil
    steps = (np.arange(BLK)[:, None] + BLK) - np.arange(2 * BLK)[None, :]
    band = (steps >= 0) & (steps <= n_back)
    dist = (np.clip(steps, 0, None) * dil).astype(np.float32)
    slopes = _alibi_slopes_grouped()[gi]
    bias = -slopes[:, None, None] * dist[None]
    bias = np.where(band[None], bias.astype(np.float64) * LOG2E, -np.inf).astype(np.float32)
    return bias.reshape(HEADS_PER_GROUP * BLK, 2 * BLK)


def _const_spec(shape):
    nd = len(shape)
    return pl.BlockSpec(shape, lambda *_: (0,) * nd, pipeline_mode=pl.Buffered(1))


def _cast_specs(weights, steps):
    in_specs, out_specs, out_shapes = [], [], []
    for w, col0 in weights:
        rows, cols = w.shape
        assert rows % (steps * BF16_SUBLANES) == 0 and col0 % LANES == 0
        in_specs.append(pl.BlockSpec((rows // steps, cols), lambda i, *_: (i, 0)))
        out_specs.append(pl.BlockSpec((rows // steps, cols - col0), lambda i, *_: (i, 0)))
        out_shapes.append(jax.ShapeDtypeStruct((rows, cols - col0), BF16))
    return in_specs, out_specs, out_shapes


def _cast_slabs(in_refs, out_refs):
    for src, dst in zip(in_refs, out_refs):
        col0 = src.shape[1] - dst.shape[1]
        dst[...] = src[:, col0:].astype(BF16)


def _residue_rows(src, tmp, dil):
    n = src.shape[0]
    if dil <= MAX_SUBLANE_STRIDE:
        for r in range(dil):
            yield r, (src[pl.ds(r, n // dil, stride=dil), :] if dil > 1 else src[...])
        return
    outer = dil // MAX_SUBLANE_STRIDE
    assert outer <= MAX_SUBLANE_STRIDE
    for r0 in range(MAX_SUBLANE_STRIDE):
        tmp[r0] = src[pl.ds(r0, n // MAX_SUBLANE_STRIDE, stride=MAX_SUBLANE_STRIDE), :]
        for r1 in range(outer):
            yield r1 * MAX_SUBLANE_STRIDE + r0, tmp[r0, pl.ds(r1, n // dil, stride=outer), :]


def _qkv_kernel(x_ref, g_ref, wf_ref, *refs, n_cast):
    cast_in, refs = refs[:n_cast], refs[n_cast:]
    out_refs, cast_out = refs[:3 * N_GROUPS], refs[3 * N_GROUPS:3 * N_GROUPS + n_cast]
    w_ref, scr, scr2 = refs[-3:]

    @pl.when(pl.program_id(0) == 0)
    def _():
        w_ref[...] = wf_ref[...].astype(BF16)

    _cast_slabs(cast_in, cast_out)
    for part in range(TM // SUB_QKV):
        rows = slice(part * SUB_QKV, (part + 1) * SUB_QKV)
        a = _rms(x_ref[rows, :], g_ref[...]).astype(BF16)
        res = jnp.dot(a, w_ref[...], preferred_element_type=F32)
        for which in range(3):
            for gi, (_, dil) in enumerate(DIL_GROUPS):
                slab = (which * ATTN_W + gi * GROUP_W) // LANES
                out = out_refs[gi * 3 + which]
                out_rows = slice(part * SUB_QKV // dil, (part + 1) * SUB_QKV // dil)
                for half in range(HALVES):
                    cols = slice((slab + half) * LANES, (slab + half + 1) * LANES)
                    if dil == 1:
                        pieces = [(0, res[:, cols])]
                    else:
                        src = scr.at[part, slab + half]
                        src[...] = res[:, cols]
                        pieces = _residue_rows(src, scr2.at[part, which * HALVES + half], dil)
                    for r, val in pieces:
                        if which == 0:
                            val = val * (HEAD_DIM ** -0.5 * LOG2E)
                        lo = r * GROUP_W + half * LANES
                        out[out_rows, lo:lo + LANES] = val.astype(BF16)


def _qkv_proj(x2, g_mix, w_in, cast_weights):
    T = x2.shape[0]
    steps = T // TM
    out_shapes, out_specs = [], []
    for _, dil in DIL_GROUPS:
        for _ in range(3):
            out_shapes.append(jax.ShapeDtypeStruct((T // dil, dil * GROUP_W), BF16))
            out_specs.append(pl.BlockSpec((TM // dil, dil * GROUP_W), lambda i: (i, 0)))
    cast_in, cast_out, cast_shapes = _cast_specs(cast_weights, steps)
    return pl.pallas_call(
        functools.partial(_qkv_kernel, n_cast=len(cast_weights)),
        grid=(steps,),
        in_specs=[pl.BlockSpec((TM, D_MODEL), lambda i: (i, 0)),
                  _const_spec((1, D_MODEL)),
                  _const_spec((D_MODEL, 3 * ATTN_W))] + cast_in,
        out_specs=out_specs + cast_out,
        out_shape=out_shapes + cast_shapes,
        scratch_shapes=[pltpu.VMEM((D_MODEL, 3 * ATTN_W), BF16),
                        pltpu.VMEM((TM // SUB_QKV, 3 * ATTN_W // LANES, SUB_QKV, LANES), F32),
                        pltpu.VMEM((TM // SUB_QKV, 3 * HALVES, MAX_SUBLANE_STRIDE,
                                    SUB_QKV // MAX_SUBLANE_STRIDE, LANES), F32)],
        compiler_params=pltpu.CompilerParams(
            dimension_semantics=("arbitrary",), vmem_limit_bytes=VMEM_LIMIT_QKV),
        name="qkv_proj",
    )(x2, g_mix, w_in, *[w for w, _ in cast_weights])


def _attn_block(q, k2, v2, bias, no_prev, head_masks, low_lanes):
    zero = jnp.zeros_like(q)
    qs = jnp.concatenate([jnp.where(m, q, zero) for m in head_masks], axis=0)
    s = lax.dot_general(qs, k2, (((1,), (1,)), ((), ())),
                        preferred_element_type=F32) + bias
    if no_prev is not None:
        s = jnp.where(no_prev, -jnp.inf, s)
    mx = jnp.max(s, axis=-1, keepdims=True)
    e = jnp.exp2(s - mx)
    den = jnp.sum(e, axis=-1, keepdims=True)
    pv = jnp.dot(e.astype(BF16), v2, preferred_element_type=F32)

    def head_rows(x, h):
        return x[h * BLK:(h + 1) * BLK]

    acc, m_rep, l_rep = [], [], []
    for half in range(HALVES):
        ha, hb = 2 * half, 2 * half + 1
        cols = slice(half * LANES, (half + 1) * LANES)
        acc.append(jnp.where(low_lanes, head_rows(pv, ha)[:, cols], head_rows(pv, hb)[:, cols]))
        m_rep.append(jnp.where(low_lanes, head_rows(mx, ha), head_rows(mx, hb)))
        l_rep.append(jnp.where(low_lanes, head_rows(den, ha), head_rows(den, hb)))
    return acc, m_rep, l_rep


def _attn_kernel(*refs):
    in_refs, (y_ref, nat, stage) = refs[:6 * N_GROUPS], refs[6 * N_GROUPS:]
    first_tile = pl.program_id(1) == 0
    lane = lax.broadcasted_iota(jnp.int32, (1, GROUP_W), 1)
    head_masks = [(lane >= HEAD_DIM * h) & (lane < HEAD_DIM * (h + 1))
                  for h in range(HEADS_PER_GROUP)]
    low_lanes = lax.broadcasted_iota(jnp.int32, (1, LANES), 1) < HEAD_DIM
    key_col = lax.broadcasted_iota(jnp.int32, (1, 2 * BLK), 1)
    no_prev_first = jnp.logical_and(first_tile, key_col < BLK)

    def group_blocks(gi):
        q_ref, k_ref, kp_ref, v_ref, vp_ref, bias_ref = in_refs[6 * gi:6 * gi + 6]
        dil = DIL_GROUPS[gi][1]
        for r in range(dil):
            lanes = slice(r * GROUP_W, (r + 1) * GROUP_W)
            for i in range(ATTN_TILE // dil // BLK):
                cur = slice(i * BLK, (i + 1) * BLK)
                if i == 0:
                    kp, vp = kp_ref[:, lanes], vp_ref[:, lanes]
                else:
                    prev = slice((i - 1) * BLK, i * BLK)
                    kp, vp = k_ref[prev, lanes], v_ref[prev, lanes]
                k2 = jnp.concatenate([kp, k_ref[cur, lanes]], axis=0)
                v2 = jnp.concatenate([vp, v_ref[cur, lanes]], axis=0)
                yield r, i, _attn_block(q_ref[cur, lanes], k2, v2, bias_ref[...],
                                        no_prev_first if i == 0 else None,
                                        head_masks, low_lanes)

    for gi in range(1, N_GROUPS):
        dil = DIL_GROUPS[gi][1]
        if dil <= MAX_SUBLANE_STRIDE:
            for r, i, stats in group_blocks(gi):
                dst = pl.ds(i * BLK * dil + r, BLK, stride=dil)
                for kind, parts in enumerate(stats):
                    for half in range(HALVES):
                        nat[gi - 1, kind, half, dst, :] = parts[half]
            continue
        outer = dil // MAX_SUBLANE_STRIDE
        for r, i, stats in group_blocks(gi):
            r1, r0 = divmod(r, MAX_SUBLANE_STRIDE)
            dst = pl.ds(i * BLK * outer + r1, BLK, stride=outer)
            for kind, parts in enumerate(stats):
                for half in range(HALVES):
                    stage[kind, half, r0, dst, :] = parts[half]
        for kind in range(N_STATS):
            for half in range(HALVES):
                for r0 in range(MAX_SUBLANE_STRIDE):
                    dst = pl.ds(r0, ATTN_TILE // MAX_SUBLANE_STRIDE, stride=MAX_SUBLANE_STRIDE)
                    nat[gi - 1, kind, half, dst, :] = stage[kind, half, r0]

    for r, i, (acc0, m0, l0) in group_blocks(0):
        rows = slice(i * BLK, (i + 1) * BLK)
        for half in range(HALVES):
            accs = [acc0[half]] + [nat[g, 0, half, rows, :] for g in range(N_GROUPS - 1)]
            ms = [m0[half]] + [nat[g, 1, half, rows, :] for g in range(N_GROUPS - 1)]
            ls = [l0[half]] + [nat[g, 2, half, rows, :] for g in range(N_GROUPS - 1)]
            m_all = jnp.maximum(jnp.maximum(ms[0], ms[1]), ms[2])
            ws = [jnp.exp2(m - m_all) for m in ms]
            num = ws[0] * accs[0] + ws[1] * accs[1] + ws[2] * accs[2]
            den = ws[0] * ls[0] + ws[1] * ls[1] + ws[2] * ls[2]
            y_ref[rows, half * LANES:(half + 1) * LANES] = (num * (1.0 / den)).astype(BF16)


def _dilated_attention(qkv, batch):
    T = qkv[0].shape[0]
    tiles_per_batch = T // ATTN_TILE // batch

    def cur_map(b, t):
        return (b * tiles_per_batch + t, 0)

    operands, in_specs = [], []
    for gi, (_, dil) in enumerate(DIL_GROUPS):
        qd, kd, vd = qkv[3 * gi:3 * gi + 3]
        rows, width = ATTN_TILE // dil, dil * GROUP_W
        prev_blocks = rows // BLK

        def prev_map(b, t, prev_blocks=prev_blocks):
            return (jnp.maximum((b * tiles_per_batch + t) * prev_blocks - 1, 0), 0)

        bias = jnp.asarray(_attn_bias(gi))
        operands += [qd, kd, kd, vd, vd, bias]
        in_specs += [pl.BlockSpec((rows, width), cur_map),
                     pl.BlockSpec((rows, width), cur_map),
                     pl.BlockSpec((BLK, width), prev_map),
                     pl.BlockSpec((rows, width), cur_map),
                     pl.BlockSpec((BLK, width), prev_map),
                     _const_spec(bias.shape)]
    return pl.pallas_call(
        _attn_kernel,
        grid=(batch, tiles_per_batch),
        in_specs=in_specs,
        out_specs=pl.BlockSpec((ATTN_TILE, GROUP_W), cur_map),
        out_shape=jax.ShapeDtypeStruct((T, GROUP_W), BF16),
        scratch_shapes=[pltpu.VMEM((N_GROUPS - 1, N_STATS, HALVES, ATTN_TILE, LANES), F32),
                        pltpu.VMEM((N_STATS, HALVES, MAX_SUBLANE_STRIDE,
                                    ATTN_TILE // MAX_SUBLANE_STRIDE, LANES), F32)],
        compiler_params=pltpu.CompilerParams(
            dimension_semantics=("arbitrary", "arbitrary"), vmem_limit_bytes=VMEM_LIMIT_ATTN),
        name="dilated_attn",
    )(*operands)


def _mixer_kernel(x_ref, y_ref, g_ref, w_ref, bg_ref, ws_ref, bs_ref, gs_ref,
                  wba_ref, wbs_ref, wo_ref, mem_ref, gm_ref, wkv_ref, gc_ref, wq_ref, wco_ref,
                  *refs, n_cast, steps_per_batch):
    cast_in, h_ref, cast_out, kv_ref = (refs[:n_cast], refs[n_cast], refs[n_cast + 1:-1],
                                        refs[-1])

    @pl.when(pl.program_id(0) % steps_per_batch == 0)
    def _():
        m = _rms(mem_ref[...], gm_ref[...]).astype(BF16)
        kv_ref[...] = jnp.dot(m, wkv_ref[...], preferred_element_type=F32).astype(BF16)

    for j in range(D_FF // FF_TILE):
        for half in range(2):
            src_cols = slice(half * D_FF + j * FF_TILE, half * D_FF + (j + 1) * FF_TILE)
            dst_cols = slice((2 * j + half) * FF_TILE, (2 * j + half + 1) * FF_TILE)
            cast_out[0][:, dst_cols] = cast_in[0][:, src_cols].astype(BF16)
    _cast_slabs(cast_in[1:], cast_out[1:])
    row = lax.broadcasted_iota(jnp.int32, (SGU_CHUNK, SGU_CHUNK), 0)
    col = lax.broadcasted_iota(jnp.int32, (SGU_CHUNK, SGU_CHUNK), 1)
    ws = [jnp.where(row >= col, ws_ref[g], 0.0).astype(BF16) for g in range(SGU_GROUPS)]
    bias = jnp.concatenate(
        [jnp.broadcast_to(jnp.sum(jnp.where(row == col, bs_ref[g:g + 1, :], 0.0),
                                  axis=1, keepdims=True), (SGU_CHUNK, SGU_GROUP_W))
         for g in range(SGU_GROUPS)], axis=1)
    for part in range(TM // SUB_MIX):
        rows = slice(part * SUB_MIX, (part + 1) * SUB_MIX)
        x = x_ref[rows, :]
        a = _rms(x, g_ref[...]).astype(BF16)
        proj = jnp.dot(a, w_ref[...], preferred_element_type=F32)

        z = jax.nn.gelu(proj[:, :2 * SGU_W], approximate=True)
        u = z[:, :SGU_W]
        v = _rms(z[:, SGU_W:], gs_ref[...]).astype(BF16)
        n_chunks = SUB_MIX // SGU_CHUNK
        mixed = []
        for g in range(SGU_GROUPS):
            gl = slice(g * SGU_GROUP_W, (g + 1) * SGU_GROUP_W)
            vg = jnp.concatenate([v[c * SGU_CHUNK:(c + 1) * SGU_CHUNK, gl]
                                  for c in range(n_chunks)], axis=1)
            mixed.append(jnp.dot(ws[g], vg, preferred_element_type=F32))
        chunks = [jnp.concatenate([m[:, c * SGU_GROUP_W:(c + 1) * SGU_GROUP_W] for m in mixed],
                                  axis=1) + bias for c in range(n_chunks)]
        y_sgu = u * jnp.concatenate(chunks, axis=0)

        gates = 0.5 * jnp.tanh(0.5 * (proj[:, 2 * SGU_W:] + bg_ref[...])) + 0.5
        merged = (gates[:, :D_MODEL] * jnp.dot(y_ref[rows, :], wba_ref[...],
                                                preferred_element_type=F32)
                  + gates[:, D_MODEL:] * jnp.dot(y_sgu.astype(BF16), wbs_ref[...],
                                                 preferred_element_type=F32))
        h = x + jnp.dot(merged.astype(BF16), wo_ref[...], preferred_element_type=F32)
        h_ref[rows, :] = _cross_rows(h, kv_ref, gc_ref, wq_ref, wco_ref)


def _cross_rows(h, kv_ref, g_ref, wq_ref, wo_ref):
    c = _rms(h, g_ref[...]).astype(BF16)
    q = jnp.dot(c, wq_ref[...], preferred_element_type=F32).astype(BF16)
    heads = []
    for hd in range(MEM_HEADS):
        cols = slice(hd * MEM_HEAD_DIM, (hd + 1) * MEM_HEAD_DIM)
        k = kv_ref[:, cols]
        v = kv_ref[:, MEM_W + hd * MEM_HEAD_DIM:MEM_W + (hd + 1) * MEM_HEAD_DIM]
        s = lax.dot_general(q[:, cols], k, (((1,), (1,)), ((), ())),
                            preferred_element_type=F32) * (MEM_HEAD_DIM ** -0.5)
        e = jnp.exp(s - jnp.max(s, axis=-1, keepdims=True))
        den = jnp.sum(e, axis=-1, keepdims=True)
        o = jnp.dot(e.astype(BF16), v, preferred_element_type=F32) * (1.0 / den)
        heads.append(o.astype(BF16))
    o = jnp.concatenate(heads, axis=1)
    return h + jnp.dot(o, wo_ref[...], preferred_element_type=F32)


def _mixer(x2, y_attn, g_mix, w_uvgl, b_gate, w_s, b_s, g_sgu, w_ba, w_bs, w_out,
           mem2, g_mem, w_kv, g_cross, w_q, w_o, cast_weights, batch):
    T = x2.shape[0]
    steps = T // TM
    steps_per_batch = steps // batch
    mem_len = mem2.shape[0] // batch
    row_spec = lambda w: pl.BlockSpec((TM, w), lambda i: (i, 0))
    consts = (g_mix, w_uvgl, b_gate, w_s, b_s, g_sgu, w_ba, w_bs, w_out)
    cross_consts = (g_mem, w_kv, g_cross, w_q, w_o)
    const_spec = lambda c: (pl.BlockSpec(memory_space=pltpu.VMEM) if c.dtype == BF16
                            else _const_spec(c.shape))
    cast_in, cast_out, cast_shapes = _cast_specs(cast_weights, steps)
    return pl.pallas_call(
        functools.partial(_mixer_kernel, n_cast=len(cast_weights),
                          steps_per_batch=steps_per_batch),
        grid=(steps,),
        in_specs=([row_spec(D_MODEL), row_spec(GROUP_W)] + [const_spec(c) for c in consts]
                  + [pl.BlockSpec((mem_len, D_MODEL), lambda i: (i // steps_per_batch, 0))]
                  + [const_spec(c) for c in cross_consts] + cast_in),
        out_specs=[row_spec(D_MODEL)] + cast_out,
        out_shape=[jax.ShapeDtypeStruct((T, D_MODEL), F32)] + cast_shapes,
        scratch_shapes=[pltpu.VMEM((mem_len, 2 * MEM_W), BF16)],
        compiler_params=pltpu.CompilerParams(
            dimension_semantics=("arbitrary",), vmem_limit_bytes=VMEM_LIMIT_MIX),
        name="mixer_cross",
    )(x2, y_attn, *consts, mem2, *cross_consts, *[w for w, _ in cast_weights])


def _ffn_kernel(h_ref, g_ref, wgu_ref, wd_ref, gf_ref, out_ref):
    for part in range(TM // SUB_FFN):
        rows = slice(part * SUB_FFN, (part + 1) * SUB_FFN)
        h = h_ref[rows, :]
        f = _rms(h, g_ref[...]).astype(BF16)
        gu = jnp.dot(f, wgu_ref[...], preferred_element_type=F32)
        tiles = []
        for j in range(D_FF // FF_TILE):
            gt = gu[:, 2 * j * FF_TILE:(2 * j + 1) * FF_TILE]
            up = gu[:, (2 * j + 1) * FF_TILE:(2 * j + 2) * FF_TILE]
            tiles.append((0.5 * gt) * (1.0 + jnp.tanh(0.5 * gt)) * up)
        act = jnp.concatenate(tiles, axis=1).astype(BF16)
        acc = h + jnp.dot(act, wd_ref[...], preferred_element_type=F32)
        out_ref[rows, :] = _rms(acc, gf_ref[...])


def _ffn(h, g_ffn, w_gu, w_down, g_final):
    T = h.shape[0]
    consts = (g_ffn, w_gu, w_down, g_final)
    return pl.pallas_call(
        _ffn_kernel,
        grid=(T // TM,),
        in_specs=[pl.BlockSpec((TM, D_MODEL), lambda i: (i, 0)),
                  _const_spec(g_ffn.shape),
                  pl.BlockSpec(memory_space=pltpu.VMEM),
                  pl.BlockSpec(memory_space=pltpu.VMEM),
                  _const_spec(g_final.shape)],
        out_specs=pl.BlockSpec((TM, D_MODEL), lambda i: (i, 0)),
        out_shape=jax.ShapeDtypeStruct((T, D_MODEL), F32),
        compiler_params=pltpu.CompilerParams(
            dimension_semantics=("arbitrary",), vmem_limit_bytes=VMEM_LIMIT_FFN),
        name="ffn",
    )(h, *consts)


def kernel(x, mem, g_mix, w_in, b_gate, w_sgu_spatial, b_sgu_spatial, g_sgu, w_branch_attn,
           w_branch_sgu, w_out, g_cross, g_mem, w_q_cross, w_kv_cross, w_o_cross, g_ffn,
           w_gate_up, w_down, g_final):
    B, S, D = x.shape
    assert D == D_MODEL and S % ATTN_TILE == 0 and w_in.shape[0] == 1
    T = B * S
    x2 = x.reshape(T, D)
    row = lambda v: v.reshape(1, -1)

    g_mix2 = row(g_mix[0])
    w_in2 = w_in.reshape(D, w_in.shape[-1])

    *qkv, w_uvgl, w_ba, w_bs, w_o1, w_q, w_kv, w_o2 = _qkv_proj(
        x2, g_mix2, w_in2,
        ((w_in2, 3 * ATTN_W), (w_branch_attn[0], 0), (w_branch_sgu[0], 0), (w_out[0], 0),
         (w_q_cross[0], 0), (w_kv_cross[0], 0), (w_o_cross[0], 0)))
    y_attn = _dilated_attention(qkv, batch=B)

    h, w_gu, w_dn = _mixer(
        x2, y_attn, g_mix2, w_uvgl, row(b_gate[0]), w_sgu_spatial[0], b_sgu_spatial[0],
        row(g_sgu[0]), w_ba, w_bs, w_o1,
        mem.reshape(B * mem.shape[1], D), row(g_mem[0]), w_kv, row(g_cross[0]), w_q, w_o2,
        ((w_gate_up[0], 0), (w_down[0], 0)), batch=B)
    out = _ffn(h, row(g_ffn[0]), w_gu, w_dn, row(g_final))
    return out.reshape(B, S, D)
```

```python
import functools
import math

import jax
import jax.numpy as jnp
import numpy as np
from jax import lax
from jax.experimental import pallas as pl
from jax.experimental.pallas import tpu as pltpu

D_MODEL = 1024
HEAD_DIM = 64
DIL_GROUPS = ((128, 1), (512, 4), (2048, 16))
N_GROUPS = 3
HEADS_PER_GROUP = 4
N_ATTN_HEADS = N_GROUPS * HEADS_PER_GROUP
ATTN_W = N_ATTN_HEADS * HEAD_DIM
GROUP_W = HEADS_PER_GROUP * HEAD_DIM
BLK = 128
LANES = 128
BF16_SUBLANES = 16
MAX_SUBLANE_STRIDE = 4
HALVES = GROUP_W // LANES
N_STATS = 3
SGU_CHUNK = 128
SGU_GROUPS = 4
SGU_W = 512
SGU_GROUP_W = SGU_W // SGU_GROUPS
MEM_HEADS = 4
MEM_HEAD_DIM = 128
MEM_W = MEM_HEADS * MEM_HEAD_DIM
D_FF = 2816
FF_TILE = 256
EPS = 1e-6
LOG2E = math.log2(math.e)

ATTN_TILE = 2048
TM = 1024
TM_CROSS = 2048
SUB_QKV = 512
SUB_MIX = 512
SUB_CROSS = 512
SUB_FFN = 256
VMEM_LIMIT_QKV = 56 * 1024 * 1024
VMEM_LIMIT_ATTN = 50 * 1024 * 1024
VMEM_LIMIT_MIX = 40 * 1024 * 1024
VMEM_LIMIT_CROSS = 40 * 1024 * 1024
VMEM_LIMIT_FFN = 28 * 1024 * 1024

F32 = jnp.float32
BF16 = jnp.bfloat16


def _rms(xf, g):
    r = lax.rsqrt(jnp.mean(xf * xf, axis=-1, keepdims=True) + EPS)
    return xf * r * g


def _alibi_slopes_grouped():
    def pow2(n):
        start = 2.0 ** (-8.0 / n)
        return [start ** (i + 1) for i in range(n)]
    n = N_ATTN_HEADS
    if math.log2(n).is_integer():
        s = pow2(n)
    else:
        c = 2 ** int(math.floor(math.log2(n)))
        s = pow2(c) + pow2(2 * c)[0::2][: n - c]
    s = np.array(sorted(s, reverse=True), dtype=np.float32)
    return s.reshape(N_GROUPS, HEADS_PER_GROUP)


def _attn_bias(gi):
    window, dil = DIL_GROUPS[gi]
    n_back = window // dil
    steps = (np.arange(BLK)[:, None] + BLK) - np.arange(2 * BLK)[None, :]
    band = (steps >= 0) & (steps <= n_back)
    dist = (np.clip(steps, 0, None) * dil).astype(np.float32)
    slopes = _alibi_slopes_grouped()[gi]
    bias = -slopes[:, None, None] * dist[None]
    bias = np.where(band[None], bias.astype(np.float64) * LOG2E, -np.inf).astype(np.float32)
    return bias.reshape(HEADS_PER_GROUP * BLK, 2 * BLK)


def _const_spec(shape):
    nd = len(shape)
    return pl.BlockSpec(shape, lambda *_: (0,) * nd, pipeline_mode=pl.Buffered(1))


def _cast_specs(weights, steps):
    in_specs, out_specs, out_shapes = [], [], []
    for w, col0 in weights:
        rows, cols = w.shape
        assert rows % (steps * BF16_SUBLANES) == 0 and col0 % LANES == 0
        in_specs.append(pl.BlockSpec((rows // steps, cols), lambda i, *_: (i, 0)))
        out_specs.append(pl.BlockSpec((rows // steps, cols - col0), lambda i, *_: (i, 0)))
        out_shapes.append(jax.ShapeDtypeStruct((rows, cols - col0), BF16))
    return in_specs, out_specs, out_shapes


def _cast_slabs(in_refs, out_refs):
    for src, dst in zip(in_refs, out_refs):
        col0 = src.shape[1] - dst.shape[1]
        dst[...] = src[:, col0:].astype(BF16)


def _residue_rows(src, tmp, dil):
    n = src.shape[0]
    if dil <= MAX_SUBLANE_STRIDE:
        for r in range(dil):
            yield r, (src[pl.ds(r, n // dil, stride=dil), :] if dil > 1 else src[...])
        return
    outer = dil // MAX_SUBLANE_STRIDE
    assert outer <= MAX_SUBLANE_STRIDE
    for r0 in range(MAX_SUBLANE_STRIDE):
        tmp[r0] = src[pl.ds(r0, n // MAX_SUBLANE_STRIDE, stride=MAX_SUBLANE_STRIDE), :]
        for r1 in range(outer):
            yield r1 * MAX_SUBLANE_STRIDE + r0, tmp[r0, pl.ds(r1, n // dil, stride=outer), :]


def _qkv_kernel(x_ref, g_ref, wf_ref, *refs, n_cast):
    cast_in, refs = refs[:n_cast], refs[n_cast:]
    out_refs, cast_out = refs[:N_GROUPS], refs[N_GROUPS:N_GROUPS + n_cast]
    w_ref, scr, scr2 = refs[-3:]

    @pl.when(pl.program_id(0) == 0)
    def _():
        w_ref[...] = wf_ref[...].astype(BF16)

    _cast_slabs(cast_in, cast_out)
    for part in range(TM // SUB_QKV):
        rows = slice(part * SUB_QKV, (part + 1) * SUB_QKV)
        a = _rms(x_ref[rows, :], g_ref[...]).astype(BF16)
        res = jnp.dot(a, w_ref[...], preferred_element_type=F32)
        for which in range(3):
            for gi, (_, dil) in enumerate(DIL_GROUPS):
                slab = (which * ATTN_W + gi * GROUP_W) // LANES
                out = out_refs[gi]
                out_rows = slice(part * SUB_QKV // dil, (part + 1) * SUB_QKV // dil)
                for half in range(HALVES):
                    cols = slice((slab + half) * LANES, (slab + half + 1) * LANES)
                    if dil == 1:
                        pieces = [(0, res[:, cols])]
                    else:
                        src = scr.at[part, slab + half]
                        src[...] = res[:, cols]
                        pieces = _residue_rows(src, scr2.at[part, which * HALVES + half], dil)
                    for r, val in pieces:
                        if which == 0:
                            val = val * (HEAD_DIM ** -0.5 * LOG2E)
                        lo = (which * dil + r) * GROUP_W + half * LANES
                        out[out_rows, lo:lo + LANES] = val.astype(BF16)


def _qkv_proj(x2, g_mix, w_in, cast_weights):
    T = x2.shape[0]
    steps = T // TM
    out_shapes, out_specs = [], []
    for _, dil in DIL_GROUPS:
        out_shapes.append(jax.ShapeDtypeStruct((T // dil, 3 * dil * GROUP_W), BF16))
        out_specs.append(pl.BlockSpec((TM // dil, 3 * dil * GROUP_W), lambda i: (i, 0)))
    cast_in, cast_out, cast_shapes = _cast_specs(cast_weights, steps)
    return pl.pallas_call(
        functools.partial(_qkv_kernel, n_cast=len(cast_weights)),
        grid=(steps,),
        in_specs=[pl.BlockSpec((TM, D_MODEL), lambda i: (i, 0)),
                  _const_spec((1, D_MODEL)),
                  _const_spec((D_MODEL, 3 * ATTN_W))] + cast_in,
        out_specs=out_specs + cast_out,
        out_shape=out_shapes + cast_shapes,
        scratch_shapes=[pltpu.VMEM((D_MODEL, 3 * ATTN_W), BF16),
                        pltpu.VMEM((TM // SUB_QKV, 3 * ATTN_W // LANES, SUB_QKV, LANES), F32),
                        pltpu.VMEM((TM // SUB_QKV, 3 * HALVES, MAX_SUBLANE_STRIDE,
                                    SUB_QKV // MAX_SUBLANE_STRIDE, LANES), F32)],
        compiler_params=pltpu.CompilerParams(
            dimension_semantics=("arbitrary",), vmem_limit_bytes=VMEM_LIMIT_QKV),
        name="qkv_proj",
    )(x2, g_mix, w_in, *[w for w, _ in cast_weights])


def _attn_block(q, k2, v2, bias, no_prev, head_masks, low_lanes):
    zero = jnp.zeros_like(q)
    qs = jnp.concatenate([jnp.where(m, q, zero) for m in head_masks], axis=0)
    s = lax.dot_general(qs, k2, (((1,), (1,)), ((), ())),
                        preferred_element_type=F32) + bias
    if no_prev is not None:
        s = jnp.where(no_prev, -jnp.inf, s)
    mx = jnp.max(s, axis=-1, keepdims=True)
    e = jnp.exp2(s - mx)
    den = jnp.sum(e, axis=-1, keepdims=True)
    pv = jnp.dot(e.astype(BF16), v2, preferred_element_type=F32)

    def head_rows(x, h):
        return x[h * BLK:(h + 1) * BLK]

    acc, m_rep, l_rep = [], [], []
    for half in range(HALVES):
        ha, hb = 2 * half, 2 * half + 1
        cols = slice(half * LANES, (half + 1) * LANES)
        acc.append(jnp.where(low_lanes, head_rows(pv, ha)[:, cols], head_rows(pv, hb)[:, cols]))
        m_rep.append(jnp.where(low_lanes, head_rows(mx, ha), head_rows(mx, hb)))
        l_rep.append(jnp.where(low_lanes, head_rows(den, ha), head_rows(den, hb)))
    return acc, m_rep, l_rep


def _attn_kernel(*refs):
    in_refs, (y_ref, nat, stage) = refs[:6 * N_GROUPS], refs[6 * N_GROUPS:]
    first_tile = pl.program_id(1) == 0
    lane = lax.broadcasted_iota(jnp.int32, (1, GROUP_W), 1)
    head_masks = [(lane >= HEAD_DIM * h) & (lane < HEAD_DIM * (h + 1))
                  for h in range(HEADS_PER_GROUP)]
    low_lanes = lax.broadcasted_iota(jnp.int32, (1, LANES), 1) < HEAD_DIM
    key_col = lax.broadcasted_iota(jnp.int32, (1, 2 * BLK), 1)
    no_prev_first = jnp.logical_and(first_tile, key_col < BLK)

    def group_blocks(gi):
        q_ref, k_ref, kp_ref, v_ref, vp_ref, bias_ref = in_refs[6 * gi:6 * gi + 6]
        dil = DIL_GROUPS[gi][1]
        for r in range(dil):
            lanes = slice(r * GROUP_W, (r + 1) * GROUP_W)
            for i in range(ATTN_TILE // dil // BLK):
                cur = slice(i * BLK, (i + 1) * BLK)
                if i == 0:
                    kp, vp = kp_ref[:, lanes], vp_ref[:, lanes]
                else:
                    prev = slice((i - 1) * BLK, i * BLK)
                    kp, vp = k_ref[prev, lanes], v_ref[prev, lanes]
                k2 = jnp.concatenate([kp, k_ref[cur, lanes]], axis=0)
                v2 = jnp.concatenate([vp, v_ref[cur, lanes]], axis=0)
                yield r, i, _attn_block(q_ref[cur, lanes], k2, v2, bias_ref[...],
                                        no_prev_first if i == 0 else None,
                                        head_masks, low_lanes)

    for gi in range(1, N_GROUPS):
        dil = DIL_GROUPS[gi][1]
        if dil <= MAX_SUBLANE_STRIDE:
            for r, i, stats in group_blocks(gi):
                dst = pl.ds(i * BLK * dil + r, BLK, stride=dil)
                for kind, parts in enumerate(stats):
                    for half in range(HALVES):
                        nat[gi - 1, kind, half, dst, :] = parts[half]
            continue
        outer = dil // MAX_SUBLANE_STRIDE
        for r, i, stats in group_blocks(gi):
            r1, r0 = divmod(r, MAX_SUBLANE_STRIDE)
            dst = pl.ds(i * BLK * outer + r1, BLK, stride=outer)
            for kind, parts in enumerate(stats):
                for half in range(HALVES):
                    stage[kind, half, r0, dst, :] = parts[half]
        for kind in range(N_STATS):
            for half in range(HALVES):
                for r0 in range(MAX_SUBLANE_STRIDE):
                    dst = pl.ds(r0, ATTN_TILE // MAX_SUBLANE_STRIDE, stride=MAX_SUBLANE_STRIDE)
                    nat[gi - 1, kind, half, dst, :] = stage[kind, half, r0]

    for r, i, (acc0, m0, l0) in group_blocks(0):
        rows = slice(i * BLK, (i + 1) * BLK)
        for half in range(HALVES):
            accs = [acc0[half]] + [nat[g, 0, half, rows, :] for g in range(N_GROUPS - 1)]
            ms = [m0[half]] + [nat[g, 1, half, rows, :] for g in range(N_GROUPS - 1)]
            ls = [l0[half]] + [nat[g, 2, half, rows, :] for g in range(N_GROUPS - 1)]
            m_all = jnp.maximum(jnp.maximum(ms[0], ms[1]), ms[2])
            ws = [jnp.exp2(m - m_all) for m in ms]
            num = ws[0] * accs[0] + ws[1] * accs[1] + ws[2] * accs[2]
            den = ws[0] * ls[0] + ws[1] * ls[1] + ws[2] * ls[2]
            y_ref[rows, half * LANES:(half + 1) * LANES] = (num * (1.0 / den)).astype(BF16)


def _dilated_attention(qkv, batch):
    T = qkv[0].shape[0]
    tiles_per_batch = T // ATTN_TILE // batch

    def cur_map(b, t, col=0):
        return (b * tiles_per_batch + t, col)

    operands, in_specs = [], []
    for gi, (_, dil) in enumerate(DIL_GROUPS):
        qkv_g = qkv[gi]
        rows, width = ATTN_TILE // dil, dil * GROUP_W
        prev_blocks = rows // BLK

        def prev_map(b, t, col, prev_blocks=prev_blocks):
            return (jnp.maximum((b * tiles_per_batch + t) * prev_blocks - 1, 0), col)

        bias = jnp.asarray(_attn_bias(gi))
        operands += [qkv_g, qkv_g, qkv_g, qkv_g, qkv_g, bias]
        in_specs += [pl.BlockSpec((rows, width), functools.partial(cur_map, col=0)),
                     pl.BlockSpec((rows, width), functools.partial(cur_map, col=1)),
                     pl.BlockSpec((BLK, width), functools.partial(prev_map, col=1)),
                     pl.BlockSpec((rows, width), functools.partial(cur_map, col=2)),
                     pl.BlockSpec((BLK, width), functools.partial(prev_map, col=2)),
                     _const_spec(bias.shape)]
    return pl.pallas_call(
        _attn_kernel,
        grid=(batch, tiles_per_batch),
        in_specs=in_specs,
        out_specs=pl.BlockSpec((ATTN_TILE, GROUP_W), cur_map),
        out_shape=jax.ShapeDtypeStruct((T, GROUP_W), BF16),
        scratch_shapes=[pltpu.VMEM((N_GROUPS - 1, N_STATS, HALVES, ATTN_TILE, LANES), F32),
                        pltpu.VMEM((N_STATS, HALVES, MAX_SUBLANE_STRIDE,
                                    ATTN_TILE // MAX_SUBLANE_STRIDE, LANES), F32)],
        compiler_params=pltpu.CompilerParams(
            dimension_semantics=("arbitrary", "arbitrary"), vmem_limit_bytes=VMEM_LIMIT_ATTN),
        name="dilated_attn",
    )(*operands)


def _mixer_kernel(x_ref, y_ref, g_ref, w_ref, bg_ref, ws_ref, bs_ref, gs_ref,
                  wba_ref, wbs_ref, wo_ref, *refs, n_cast):
    cast_in, h_ref, cast_out = refs[:n_cast], refs[n_cast], refs[n_cast + 1:]
    for j in range(D_FF // FF_TILE):
        for half in range(2):
            src_cols = slice(half * D_FF + j * FF_TILE, half * D_FF + (j + 1) * FF_TILE)
            dst_cols = slice((2 * j + half) * FF_TILE, (2 * j + half + 1) * FF_TILE)
            cast_out[0][:, dst_cols] = cast_in[0][:, src_cols].astype(BF16)
    _cast_slabs(cast_in[1:], cast_out[1:])
    row = lax.broadcasted_iota(jnp.int32, (SGU_CHUNK, SGU_CHUNK), 0)
    col = lax.broadcasted_iota(jnp.int32, (SGU_CHUNK, SGU_CHUNK), 1)
    ws = [jnp.where(row >= col, ws_ref[g], 0.0).astype(BF16) for g in range(SGU_GROUPS)]
    bias = jnp.concatenate(
        [jnp.broadcast_to(jnp.sum(jnp.where(row == col, bs_ref[g:g + 1, :], 0.0),
                                  axis=1, keepdims=True), (SGU_CHUNK, SGU_GROUP_W))
         for g in range(SGU_GROUPS)], axis=1)
    for part in range(TM // SUB_MIX):
        rows = slice(part * SUB_MIX, (part + 1) * SUB_MIX)
        x = x_ref[rows, :]
        a = _rms(x, g_ref[...]).astype(BF16)
        proj = jnp.dot(a, w_ref[...], preferred_element_type=F32)

        z = jax.nn.gelu(proj[:, :2 * SGU_W], approximate=True)
        u = z[:, :SGU_W]
        v = _rms(z[:, SGU_W:], gs_ref[...]).astype(BF16)
        n_chunks = SUB_MIX // SGU_CHUNK
        mixed = []
        for g in range(SGU_GROUPS):
            gl = slice(g * SGU_GROUP_W, (g + 1) * SGU_GROUP_W)
            vg = jnp.concatenate([v[c * SGU_CHUNK:(c + 1) * SGU_CHUNK, gl]
                                  for c in range(n_chunks)], axis=1)
            mixed.append(jnp.dot(ws[g], vg, preferred_element_type=F32))
        chunks = [jnp.concatenate([m[:, c * SGU_GROUP_W:(c + 1) * SGU_GROUP_W] for m in mixed],
                                  axis=1) + bias for c in range(n_chunks)]
        y_sgu = u * jnp.concatenate(chunks, axis=0)

        gates = 0.5 * jnp.tanh(0.5 * (proj[:, 2 * SGU_W:] + bg_ref[...])) + 0.5
        merged = (gates[:, :D_MODEL] * jnp.dot(y_ref[rows, :], wba_ref[...],
                                                preferred_element_type=F32)
                  + gates[:, D_MODEL:] * jnp.dot(y_sgu.astype(BF16), wbs_ref[...],
                                                 preferred_element_type=F32))
        h_ref[rows, :] = x + jnp.dot(merged.astype(BF16), wo_ref[...],
                                     preferred_element_type=F32)


def _mixer(x2, y_attn, g_mix, w_uvgl, b_gate, w_s, b_s, g_sgu, w_ba, w_bs, w_out, cast_weights):
    T = x2.shape[0]
    steps = T // TM
    row_spec = lambda w: pl.BlockSpec((TM, w), lambda i: (i, 0))
    consts = (g_mix, w_uvgl, b_gate, w_s, b_s, g_sgu, w_ba, w_bs, w_out)
    const_specs = [pl.BlockSpec(memory_space=pltpu.VMEM) if c.dtype == BF16
                   else _const_spec(c.shape) for c in consts]
    cast_in, cast_out, cast_shapes = _cast_specs(cast_weights, steps)
    return pl.pallas_call(
        functools.partial(_mixer_kernel, n_cast=len(cast_weights)),
        grid=(steps,),
        in_specs=[row_spec(D_MODEL), row_spec(GROUP_W)] + const_specs + cast_in,
        out_specs=[row_spec(D_MODEL)] + cast_out,
        out_shape=[jax.ShapeDtypeStruct((T, D_MODEL), F32)] + cast_shapes,
        compiler_params=pltpu.CompilerParams(
            dimension_semantics=("arbitrary",), vmem_limit_bytes=VMEM_LIMIT_MIX),
        name="mixer_tail",
    )(x2, y_attn, *consts, *[w for w, _ in cast_weights])


def _cross_kernel(h_ref, mem_ref, gm_ref, wkv_ref, g_ref, wq_ref, wo_ref, out_ref, kv_ref):
    @pl.when(pl.program_id(1) == 0)
    def _():
        m = _rms(mem_ref[...], gm_ref[...]).astype(BF16)
        kv_ref[...] = jnp.dot(m, wkv_ref[...], preferred_element_type=F32).astype(BF16)

    for part in range(TM_CROSS // SUB_CROSS):
        rows = slice(part * SUB_CROSS, (part + 1) * SUB_CROSS)
        h = h_ref[rows, :]
        c = _rms(h, g_ref[...]).astype(BF16)
        q = jnp.dot(c, wq_ref[...], preferred_element_type=F32).astype(BF16)
        heads = []
        for hd in range(MEM_HEADS):
            cols = slice(hd * MEM_HEAD_DIM, (hd + 1) * MEM_HEAD_DIM)
            k = kv_ref[:, cols]
            v = kv_ref[:, MEM_W + hd * MEM_HEAD_DIM:MEM_W + (hd + 1) * MEM_HEAD_DIM]
            s = lax.dot_general(q[:, cols], k, (((1,), (1,)), ((), ())),
                                preferred_element_type=F32) * (MEM_HEAD_DIM ** -0.5)
            e = jnp.exp(s - jnp.max(s, axis=-1, keepdims=True))
            den = jnp.sum(e, axis=-1, keepdims=True)
            o = jnp.dot(e.astype(BF16), v, preferred_element_type=F32) * (1.0 / den)
            heads.append(o.astype(BF16))
        o = jnp.concatenate(heads, axis=1)
        out_ref[rows, :] = h + jnp.dot(o, wo_ref[...], preferred_element_type=F32)


def _cross_attention(h, mem2, g_mem, w_kv, g_cross, w_q, w_o, batch):
    T = h.shape[0]
    blocks_per_batch = T // batch // TM_CROSS
    mem_len = mem2.shape[0] // batch
    consts = (g_mem, w_kv, g_cross, w_q, w_o)
    row_map = lambda b, i: (b * blocks_per_batch + i, 0)
    return pl.pallas_call(
        _cross_kernel,
        grid=(batch, blocks_per_batch),
        in_specs=[pl.BlockSpec((TM_CROSS, D_MODEL), row_map),
                  pl.BlockSpec((mem_len, D_MODEL), lambda b, i: (b, 0))]
                 + [_const_spec(c.shape) for c in consts],
        out_specs=pl.BlockSpec((TM_CROSS, D_MODEL), row_map),
        out_shape=jax.ShapeDtypeStruct((T, D_MODEL), F32),
        scratch_shapes=[pltpu.VMEM((mem_len, 2 * MEM_W), BF16)],
        compiler_params=pltpu.CompilerParams(
            dimension_semantics=("arbitrary", "arbitrary"), vmem_limit_bytes=VMEM_LIMIT_CROSS),
        name="cross_attn",
    )(h, mem2, *consts)


def _ffn_kernel(h_ref, g_ref, wgu_ref, wd_ref, gf_ref, out_ref):
    for part in range(TM // SUB_FFN):
        rows = slice(part * SUB_FFN, (part + 1) * SUB_FFN)
        h = h_ref[rows, :]
        f = _rms(h, g_ref[...]).astype(BF16)
        gu = jnp.dot(f, wgu_ref[...], preferred_element_type=F32)
        tiles = []
        for j in range(D_FF // FF_TILE):
            gt = gu[:, 2 * j * FF_TILE:(2 * j + 1) * FF_TILE]
            up = gu[:, (2 * j + 1) * FF_TILE:(2 * j + 2) * FF_TILE]
            tiles.append((0.5 * gt) * (1.0 + jnp.tanh(0.5 * gt)) * up)
        act = jnp.concatenate(tiles, axis=1).astype(BF16)
        acc = h + jnp.dot(act, wd_ref[...], preferred_element_type=F32)
        out_ref[rows, :] = _rms(acc, gf_ref[...])


def _ffn(h, g_ffn, w_gu, w_down, g_final):
    T = h.shape[0]
    consts = (g_ffn, w_gu, w_down, g_final)
    return pl.pallas_call(
        _ffn_kernel,
        grid=(T // TM,),
        in_specs=[pl.BlockSpec((TM, D_MODEL), lambda i: (i, 0)),
                  _const_spec(g_ffn.shape),
                  pl.BlockSpec(memory_space=pltpu.VMEM),
                  pl.BlockSpec(memory_space=pltpu.VMEM),
                  _const_spec(g_final.shape)],
        out_specs=pl.BlockSpec((TM, D_MODEL), lambda i: (i, 0)),
        out_shape=jax.ShapeDtypeStruct((T, D_MODEL), F32),
        compiler_params=pltpu.CompilerParams(
            dimension_semantics=("arbitrary",), vmem_limit_bytes=VMEM_LIMIT_FFN),
        name="ffn",
    )(h, *consts)


def kernel(x, mem, g_mix, w_in, b_gate, w_sgu_spatial, b_sgu_spatial, g_sgu, w_branch_attn,
           w_branch_sgu, w_out, g_cross, g_mem, w_q_cross, w_kv_cross, w_o_cross, g_ffn,
           w_gate_up, w_down, g_final):
    B, S, D = x.shape
    assert D == D_MODEL and S % ATTN_TILE == 0 and w_in.shape[0] == 1
    T = B * S
    x2 = x.reshape(T, D)
    row = lambda v: v.reshape(1, -1)

    g_mix2 = row(g_mix[0])
    w_in2 = w_in.reshape(D, w_in.shape[-1])

    *qkv, w_uvgl, w_ba, w_bs, w_o1, w_q, w_kv, w_o2 = _qkv_proj(
        x2, g_mix2, w_in2,
        ((w_in2, 3 * ATTN_W), (w_branch_attn[0], 0), (w_branch_sgu[0], 0), (w_out[0], 0),
         (w_q_cross[0], 0), (w_kv_cross[0], 0), (w_o_cross[0], 0)))
    y_attn = _dilated_attention(qkv, batch=B)

    h, w_gu, w_dn = _mixer(
        x2, y_attn, g_mix2, w_uvgl, row(b_gate[0]), w_sgu_spatial[0], b_sgu_spatial[0],
        row(g_sgu[0]), w_ba, w_bs, w_o1, ((w_gate_up[0], 0), (w_down[0], 0)))

    h = _cross_attention(h, mem.reshape(B * mem.shape[1], D), row(g_mem[0]), w_kv,
                         row(g_cross[0]), w_q, w_o2, batch=B)
    out = _ffn(h, row(g_ffn[0]), w_gu, w_dn, row(g_final))
    return out.reshape(B, S, D)
```
